```python
import jax, jax.numpy as jnp
from jax import lax
import numpy as np

D_MODEL = 1024
BATCH = 32
SEQ = 2048
DEPTH = 2

MIX_WIDTH = D_MODEL
CONV_WIDTH = MIX_WIDTH // 2
CONV_HEADS = 8
POOL_WIDTH = MIX_WIDTH - CONV_WIDTH
CONV_K = 3
POOL_WINDOWS = (2, 4, 8, 16)
N_POOL_GROUPS = len(POOL_WINDOWS)
POOL_GROUP = POOL_WIDTH // N_POOL_GROUPS
IN_COLS = 3 * CONV_WIDTH + POOL_WIDTH
N_EXPERTS = 16
N_EXPERT_GROUPS = 4
EXPERTS_PER_GROUP = N_EXPERTS // N_EXPERT_GROUPS
TOP_K = 2
D_EXPERT = D_MODEL // 2
EPS = 1e-6

kernel_name = "hybrid_conv_pool_grouped_moe_block"


def rmsnorm(x, g):
    xf = x.astype(jnp.float32)
    y = xf * lax.rsqrt(jnp.mean(xf * xf, axis=-1, keepdims=True) + EPS)
    return (y * g.astype(jnp.float32)).astype(x.dtype)


def modulate(h, shift, scale):
    return h * (1.0 + scale) + shift


def short_gated_conv(u_b, u_c, u_h, w):
    g = u_c * u_h
    gp = jnp.pad(g, ((0, 0), (CONV_K - 1, 0), (0, 0)))
    s = g.shape[1]
    conv = w[0] * gp[:, 0:s] + w[1] * gp[:, 1:s + 1] + w[2] * gp[:, 2:s + 2]
    return u_b * conv


def multiscale_pool(v, pool_w, pool_scale):
    s = v.shape[1]
    vf = v.astype(jnp.float32)
    cs = jnp.cumsum(vf, axis=1)
    t = jnp.arange(1, s + 1)
    groups = []
    for gi, win in enumerate(POOL_WINDOWS):
        sl = slice(gi * POOL_GROUP, (gi + 1) * POOL_GROUP)
        cs_g = cs[..., sl]
        lag = jnp.pad(cs_g, ((0, 0), (win, 0), (0, 0)))[:, :s]
        cnt = jnp.minimum(t, win).astype(jnp.float32)[:, None]
        groups.append((cs_g - lag) / cnt - vf[..., sl])
    pooled = jnp.stack(groups, axis=2).astype(v.dtype)
    y = jnp.einsum('bsgc,gcd->bsgd', pooled, pool_w)
    y = y.reshape(v.shape[0], s, POOL_WIDTH)
    return y * pool_scale


def route(h, w_router, router_bias):
    logits = jnp.einsum('bsd,de->bse', h, w_router).astype(jnp.float32)
    s = jax.nn.sigmoid(logits)
    sel = s + router_bias.astype(jnp.float32)
    sel_g = sel.reshape(*sel.shape[:-1], N_EXPERT_GROUPS, EXPERTS_PER_GROUP)
    gscore = lax.top_k(sel_g, TOP_K)[0].sum(-1)
    best = jnp.argmax(gscore, axis=-1)
    in_group = (jnp.arange(N_EXPERTS) // EXPERTS_PER_GROUP) == best[..., None]
    masked = jnp.where(in_group, sel, -jnp.inf)
    _, idx = lax.top_k(masked, TOP_K)
    w = jnp.take_along_axis(s, idx, axis=-1)
    w = w / jnp.sum(w, axis=-1, keepdims=True)
    combine = jnp.sum(jax.nn.one_hot(idx, N_EXPERTS, dtype=jnp.float32) * w[..., None], axis=-2)
    return combine


def moe(h, w_router, router_bias, w_gate_up, w_down):
    combine = route(h, w_router, router_bias).astype(h.dtype)
    out = jnp.zeros_like(h)
    for e in range(N_EXPERTS):
        gu = jnp.einsum('bsd,df->bsf', h, w_gate_up[e])
        g, u = jnp.split(gu, 2, axis=-1)
        y = jnp.einsum('bsf,fd->bsd', jax.nn.silu(g) * u, w_down[e])
        out = out + combine[..., e:e + 1] * y
    return out


def setup_inputs(seed: int = 0) -> dict:
    key = jax.random.key(seed)
    ks = jax.random.split(key, 20)
    f32 = jnp.float32
    d = D_MODEL
    nrm = lambda k, shape, sc: jax.random.normal(k, shape, f32) * sc
    return {
        "x": nrm(ks[0], (BATCH, SEQ, d), 1.0),
        "c": nrm(ks[1], (BATCH, d), 1.0),
        "w_in": nrm(ks[2], (DEPTH, d, IN_COLS), d ** -0.5),
        "conv_w": nrm(ks[3], (DEPTH, CONV_K, CONV_WIDTH), CONV_K ** -0.5),
        "pool_w": nrm(ks[4], (DEPTH, N_POOL_GROUPS, POOL_GROUP, POOL_GROUP), POOL_GROUP ** -0.5),
        "pool_scale": 1.0 + nrm(ks[5], (DEPTH, POOL_WIDTH), 0.1),
        "w_out": nrm(ks[6], (DEPTH, MIX_WIDTH, d), MIX_WIDTH ** -0.5),
        "norm1_g": 1.0 + nrm(ks[7], (DEPTH, d), 0.05),
        "norm2_g": 1.0 + nrm(ks[8], (DEPTH, d), 0.05),
        "w_ada": nrm(ks[9], (DEPTH, d, 6 * d), 0.5 * d ** -0.5),
        "b_ada": nrm(ks[10], (DEPTH, 6 * d), 0.02),
        "w_router": nrm(ks[11], (d, N_EXPERTS), d ** -0.5),
        "router_bias": nrm(ks[12], (N_EXPERTS,), 0.01),
        "w_gate_up": nrm(ks[13], (DEPTH, N_EXPERTS, d, 2 * D_EXPERT), d ** -0.5),
        "w_down": nrm(ks[14], (DEPTH, N_EXPERTS, D_EXPERT, d), D_EXPERT ** -0.5),
        "final_g": 1.0 + nrm(ks[15], (d,), 0.05),
    }


def reference(x, c, w_in, conv_w, pool_w, pool_scale, w_out, norm1_g, norm2_g,
              w_ada, b_ada, w_router, router_bias, w_gate_up, w_down, final_g):
    c_act = jax.nn.silu(c)
    for l in range(DEPTH):
        mod = jnp.einsum('bd,de->be', c_act, w_ada[l]) + b_ada[l]
        sh1, sc1, g1, sh2, sc2, g2 = [m[:, None, :] for m in jnp.split(mod, 6, axis=-1)]

        h = modulate(rmsnorm(x, norm1_g[l]), sh1, sc1)
        proj = jnp.einsum('bsd,de->bse', h, w_in[l])
        u_b = proj[..., 0:CONV_WIDTH]
        u_c = proj[..., CONV_WIDTH:2 * CONV_WIDTH]
        u_h = proj[..., 2 * CONV_WIDTH:3 * CONV_WIDTH]
        v = proj[..., 3 * CONV_WIDTH:]
        y_conv = short_gated_conv(u_b, u_c, u_h, conv_w[l])
        y_pool = multiscale_pool(v, pool_w[l], pool_scale[l])
        y_mix = jnp.concatenate([y_conv, y_pool], axis=-1)
        x = x + g1 * jnp.einsum('bse,ed->bsd', y_mix, w_out[l])

        h2 = modulate(rmsnorm(x, norm2_g[l]), sh2, sc2)
        x = x + g2 * moe(h2, w_router, router_bias, w_gate_up[l], w_down[l])
    return rmsnorm(x, final_g)
```

```python
import functools

import jax
import jax.numpy as jnp
from jax import lax
from jax.experimental import pallas as pl
from jax.experimental.pallas import tpu as pltpu

D_MODEL = 1024
CONV_WIDTH = 512
POOL_WIDTH = 512
IN_COLS = 3 * CONV_WIDTH + POOL_WIDTH
POOL_WINDOWS = (2, 4, 8, 16)
POOL_GROUP = 128
N_EXPERTS = 16
N_GROUPS = 4
GROUP_SIZE = 4
D_EXPERT = 512
EPS = 1e-6

LANES = 128
SUBLANES = 8
ROW_WORDS = SUBLANES * LANES
PACKED_SLABS = D_MODEL // 2 // LANES
INFO_SLAB = PACKED_SLABS
CONV_HALO = SUBLANES
POOL_HALO = 2 * SUBLANES

PAIR_A = (0, 0, 0, 1, 1, 2)
PAIR_B = (1, 2, 3, 2, 3, 3)
N_PAIRS = len(PAIR_A)
N_CLASSES = N_GROUPS * N_PAIRS

MIX_TOKENS = 512
FFN_ROWS = 256
RET_TOKENS = 256
DISPATCH_TOKENS = 4096
VMEM_LIMIT = 56 * 1024 * 1024

F32 = jnp.float32
BF16 = jnp.bfloat16
U32 = jnp.uint32


def _norm_mod(v, gain, shift, scale):
    ms = jnp.mean(v * v, axis=-1, keepdims=True)
    y = v * lax.rsqrt(ms + EPS) * gain
    return y * (1.0 + scale) + shift


def _ada_kernel(c_ref, w_ref, b_ref, o_ref):
    c = c_ref[...]
    c_act = (c * jax.nn.sigmoid(c)).astype(BF16)
    o_ref[0] = jnp.dot(c_act, w_ref[0].astype(BF16), preferred_element_type=F32) + b_ref[0]


def _ada(c, w_ada, b_ada):
    depth = w_ada.shape[0]
    batch = c.shape[0]
    n_col = w_ada.shape[2] // D_MODEL
    return pl.pallas_call(
        _ada_kernel,
        out_shape=jax.ShapeDtypeStruct((depth, batch, n_col * D_MODEL), F32),
        grid=(depth, n_col),
        in_specs=[
            pl.BlockSpec((batch, D_MODEL), lambda l, j: (0, 0)),
            pl.BlockSpec((1, D_MODEL, D_MODEL), lambda l, j: (l, 0, j)),
            pl.BlockSpec((1, 1, D_MODEL), lambda l, j: (l, 0, j)),
        ],
        out_specs=pl.BlockSpec((1, batch, D_MODEL), lambda l, j: (l, 0, j)),
        compiler_params=pltpu.CompilerParams(
            dimension_semantics=("arbitrary", "arbitrary"), vmem_limit_bytes=VMEM_LIMIT),
        name="ada",
    )(c, w_ada, b_ada.reshape(depth, 1, -1))


def _route(logits_t, bias):
    s = jax.nn.sigmoid(logits_t)
    sel = s + bias
    sel_r = [sel[e:e + 1] for e in range(N_EXPERTS)]
    s_r = [s[e:e + 1] for e in range(N_EXPERTS)]

    def group_score(g):
        r = sel_r[GROUP_SIZE * g:GROUP_SIZE * (g + 1)]
        best = r[PAIR_A[0]] + r[PAIR_B[0]]
        for p in range(1, N_PAIRS):
            best = jnp.maximum(best, r[PAIR_A[p]] + r[PAIR_B[p]])
        return best

    best_v = group_score(0)
    best_g = jnp.zeros_like(best_v)
    for g in range(1, N_GROUPS):
        gs = group_score(g)
        upd = gs > best_v
        best_v = jnp.where(upd, gs, best_v)
        best_g = jnp.where(upd, float(g), best_g)

    def pick(rows, i):
        out = rows[i]
        for g in range(1, N_GROUPS):
            out = jnp.where(best_g == float(g), rows[GROUP_SIZE * g + i], out)
        return out

    selg = [pick(sel_r, i) for i in range(GROUP_SIZE)]
    sg = [pick(s_r, i) for i in range(GROUP_SIZE)]
    chosen = []
    for i in range(GROUP_SIZE):
        beaten = jnp.zeros_like(best_v)
        for j in range(GROUP_SIZE):
            if j == i:
                continue
            wins = selg[j] > selg[i]
            if j < i:
                wins = wins | (selg[j] == selg[i])
            beaten = beaten + wins.astype(F32)
        chosen.append(beaten < 2.0)
    m0, m1, m2, m3 = chosen
    pair = jnp.where(m0, jnp.where(m1, 0.0, jnp.where(m2, 1.0, 2.0)),
                     jnp.where(m1, jnp.where(m2, 3.0, 4.0), 5.0))
    s_a = jnp.where(m0, sg[0], jnp.where(m1, sg[1], sg[2]))
    s_b = jnp.where(m3, sg[3], jnp.where(m2, sg[2], sg[1]))
    tot = s_a + s_b
    cls = best_g * float(N_PAIRS) + pair
    zero = jnp.zeros_like(cls)
    return jnp.concatenate([cls, s_a / tot, s_b / tot] + [zero] * (SUBLANES - 3), axis=0)


def _mix_kernel(x_ref, mod_ref, n1_ref, n2_ref, win_ref, cw_ref, pw_ref, ps_ref, wout_ref,
                wr_ref, rb_ref, x1_ref, h_ref, ri_ref, gbuf, vbuf):
    ts = x_ref.shape[1]
    st = pl.program_id(1)
    x = x_ref[0]
    mod = mod_ref[0]
    sh1, sc1, g1, sh2, sc2 = (mod[i:i + 1] for i in range(5))

    h = _norm_mod(x, n1_ref[...], sh1, sc1).astype(BF16)
    proj = jnp.dot(h, win_ref[...], preferred_element_type=F32)
    u_b = proj[:, 0:CONV_WIDTH]
    u_c = proj[:, CONV_WIDTH:2 * CONV_WIDTH]
    u_h = proj[:, 2 * CONV_WIDTH:3 * CONV_WIDTH]
    v = proj[:, 3 * CONV_WIDTH:]

    @pl.when(st == 0)
    def _():
        gbuf[0:CONV_HALO] = jnp.zeros((CONV_HALO, CONV_WIDTH), F32)
        vbuf[0:POOL_HALO] = jnp.zeros((POOL_HALO, POOL_WIDTH), F32)

    g = u_c * u_h
    gbuf[CONV_HALO:CONV_HALO + ts] = g
    vbuf[POOL_HALO:POOL_HALO + ts] = v

    cw = cw_ref[...]
    conv = (cw[0:1] * gbuf[CONV_HALO - 2:CONV_HALO - 2 + ts]
            + cw[1:2] * gbuf[CONV_HALO - 1:CONV_HALO - 1 + ts]
            + cw[2:3] * g)
    y_conv = u_b * conv

    t_pos = (st * ts + lax.broadcasted_iota(jnp.int32, (ts, 1), 0) + 1).astype(F32)
    pooled = []
    for gi, win in enumerate(POOL_WINDOWS):
        lo = gi * POOL_GROUP
        v_g = v[:, lo:lo + POOL_GROUP]
        acc = v_g
        for k in range(1, win):
            acc = acc + vbuf[POOL_HALO - k:POOL_HALO - k + ts, lo:lo + POOL_GROUP]
        inv_cnt = 1.0 / jnp.minimum(t_pos, float(win))
        pooled.append(acc * inv_cnt - v_g)
    pooled = jnp.concatenate(pooled, axis=1).astype(BF16)
    half = POOL_WIDTH // 2
    y_pool = jnp.concatenate(
        [jnp.dot(pooled[:, 0:half], pw_ref[0], preferred_element_type=F32),
         jnp.dot(pooled[:, half:], pw_ref[1], preferred_element_type=F32)], axis=1) * ps_ref[...]

    gbuf[0:CONV_HALO] = gbuf[ts:ts + CONV_HALO]
    vbuf[0:POOL_HALO] = vbuf[ts:ts + POOL_HALO]

    y_mix = jnp.concatenate([y_conv, y_pool], axis=1).astype(BF16)
    x1 = x + g1 * jnp.dot(y_mix, wout_ref[...], preferred_element_type=F32)
    x1_ref[0] = x1

    h2 = _norm_mod(x1, n2_ref[...], sh2, sc2)
    logits_t = lax.dot_general(wr_ref[...], h2.astype(BF16), (((1,), (1,)), ((), ())),
                               preferred_element_type=F32)
    info = _route(logits_t, rb_ref[...])
    ri_ref[...] = info

    packed = pltpu.pack_elementwise([h2[:, :D_MODEL // 2], h2[:, D_MODEL // 2:]], packed_dtype=BF16)
    for j in range(PACKED_SLABS):
        h_ref[pl.ds(j, ts, stride=SUBLANES), :] = packed[:, j * LANES:(j + 1) * LANES]
    info_t = jnp.concatenate([info, jnp.zeros((LANES - SUBLANES, ts), F32)], axis=0).T
    h_ref[pl.ds(INFO_SLAB, ts, stride=SUBLANES), :] = lax.bitcast_convert_type(info_t, U32)
    for j in range(INFO_SLAB + 1, SUBLANES):
        h_ref[pl.ds(j, ts, stride=SUBLANES), :] = jnp.zeros((ts, LANES), U32)


def _mix(x, mod, n1, n2, win, cw, pw, ps, wout, wr, rb):
    batch, seq, _ = x.shape
    ts = MIX_TOKENS
    n_s = seq // ts
    tokens = batch * seq
    const = lambda shape: pl.BlockSpec(shape, lambda b, s: (0,) * len(shape))
    return pl.pallas_call(
        _mix_kernel,
        out_shape=(
            jax.ShapeDtypeStruct((batch, seq, D_MODEL), F32),
            jax.ShapeDtypeStruct((tokens * SUBLANES, LANES), U32),
            jax.ShapeDtypeStruct((SUBLANES, tokens), F32),
        ),
        grid=(batch, n_s),
        in_specs=[
            pl.BlockSpec((1, ts, D_MODEL), lambda b, s: (b, s, 0)),
            pl.BlockSpec((1, 6, D_MODEL), lambda b, s: (b, 0, 0)),
            const((1, D_MODEL)), const((1, D_MODEL)),
            const((D_MODEL, IN_COLS)),
            const((3, CONV_WIDTH)),
            const((2, POOL_WIDTH // 2, POOL_WIDTH // 2)),
            const((1, POOL_WIDTH)),
            const((D_MODEL, D_MODEL)),
            const((N_EXPERTS, D_MODEL)),
            const((N_EXPERTS, 1)),
        ],
        out_specs=(
            pl.BlockSpec((1, ts, D_MODEL), lambda b, s: (b, s, 0)),
            pl.BlockSpec((ts * SUBLANES, LANES), lambda b, s: (b * n_s + s, 0)),
            pl.BlockSpec((SUBLANES, ts), lambda b, s: (0, b * n_s + s)),
        ),
        scratch_shapes=[
            pltpu.VMEM((ts + CONV_HALO, CONV_WIDTH), F32),
            pltpu.VMEM((ts + POOL_HALO, POOL_WIDTH), F32),
        ],
        compiler_params=pltpu.CompilerParams(
            dimension_semantics=("arbitrary", "arbitrary"), vmem_limit_bytes=VMEM_LIMIT),
        name="mix",
    )(x, mod, n1, n2, win, cw, pw, ps, wout, wr, rb)


def _row_copy(src, src_row, dst, dst_row, sem):
    return pltpu.make_async_copy(
        src.at[pl.ds(pl.multiple_of(src_row * SUBLANES, SUBLANES), SUBLANES)],
        dst.at[pl.ds(pl.multiple_of(dst_row * SUBLANES, SUBLANES), SUBLANES)], sem)


def _dispatch_kernel(pos_ref, h_hbm, init_hbm, xs_hbm, sem):
    del init_hbm
    n = DISPATCH_TOKENS
    base = pl.program_id(0) * n

    def issue(r, carry):
        t = base + r
        _row_copy(h_hbm, t, xs_hbm, pos_ref[t], sem).start()
        return carry

    lax.fori_loop(0, n, issue, 0, unroll=8)
    pltpu.make_async_copy(h_hbm.at[pl.ds(0, n * SUBLANES)],
                          xs_hbm.at[pl.ds(0, n * SUBLANES)], sem).wait()


def _dispatch(pos, h_rows, n_rows):
    tokens = pos.shape[0]
    init = jnp.zeros((n_rows * SUBLANES, LANES), U32)
    return pl.pallas_call(
        _dispatch_kernel,
        out_shape=jax.ShapeDtypeStruct((n_rows * SUBLANES, LANES), U32),
        grid_spec=pltpu.PrefetchScalarGridSpec(
            num_scalar_prefetch=1,
            grid=(tokens // DISPATCH_TOKENS,),
            in_specs=[pl.BlockSpec(memory_space=pl.ANY), pl.BlockSpec(memory_space=pl.ANY)],
            out_specs=pl.BlockSpec(memory_space=pl.ANY),
            scratch_shapes=[pltpu.SemaphoreType.DMA],
        ),
        input_output_aliases={2: 0},
        compiler_params=pltpu.CompilerParams(dimension_semantics=("arbitrary",)),
        name="dispatch",
    )(pos, h_rows, init)


def _ffn_kernel(ea_ref, eb_ref, valid_ref, xs_ref, wga_ref, wgb_ref, wda_ref, wdb_ref, y_ref):
    del ea_ref, eb_ref
    tm = FFN_ROWS
    i = pl.program_id(0)

    @pl.when(valid_ref[i] != 0)
    def _():
        slabs = [xs_ref[pl.ds(j, tm, stride=SUBLANES), :] for j in range(INFO_SLAB + 1)]
        packed = jnp.concatenate(slabs[:PACKED_SLABS], axis=1)
        xg = jnp.concatenate(
            [pltpu.unpack_elementwise(packed, index=k, packed_dtype=BF16, unpacked_dtype=F32)
             for k in range(2)], axis=1).astype(BF16)
        info = lax.bitcast_convert_type(slabs[INFO_SLAB], F32)

        def act(wg_ref, w):
            gu = jnp.dot(xg, wg_ref[0], preferred_element_type=F32)
            gate = gu[:, :D_EXPERT]
            up = gu[:, D_EXPERT:]
            return (gate * jax.nn.sigmoid(gate) * up * w).astype(BF16)

        y = (jnp.dot(act(wga_ref, info[:, 1:2]), wda_ref[0], preferred_element_type=F32)
             + jnp.dot(act(wgb_ref, info[:, 2:3]), wdb_ref[0], preferred_element_type=F32))
        for j in range(SUBLANES):
            y_ref[pl.ds(j, tm, stride=SUBLANES), :] = y[:, j * LANES:(j + 1) * LANES]

    @pl.when(valid_ref[i] == 0)
    def _():
        y_ref[...] = jnp.zeros(y_ref.shape, F32)


def _ffn(tile_ea, tile_eb, tile_valid, xs_rows, wgu, wd):
    n_tiles = tile_ea.shape[0]
    tm = FFN_ROWS
    return pl.pallas_call(
        _ffn_kernel,
        out_shape=jax.ShapeDtypeStruct((n_tiles * tm * SUBLANES, LANES), F32),
        grid_spec=pltpu.PrefetchScalarGridSpec(
            num_scalar_prefetch=3,
            grid=(n_tiles,),
            in_specs=[
                pl.BlockSpec((tm * SUBLANES, LANES), lambda i, ea, eb, va: (i, 0)),
                pl.BlockSpec((1, D_MODEL, 2 * D_EXPERT), lambda i, ea, eb, va: (ea[i], 0, 0)),
                pl.BlockSpec((1, D_MODEL, 2 * D_EXPERT), lambda i, ea, eb, va: (eb[i], 0, 0)),
                pl.BlockSpec((1, D_EXPERT, D_MODEL), lambda i, ea, eb, va: (ea[i], 0, 0)),
                pl.BlockSpec((1, D_EXPERT, D_MODEL), lambda i, ea, eb, va: (eb[i], 0, 0)),
            ],
            out_specs=pl.BlockSpec((tm * SUBLANES, LANES), lambda i, ea, eb, va: (i, 0)),
        ),
        compiler_params=pltpu.CompilerParams(
            dimension_semantics=("arbitrary",), vmem_limit_bytes=VMEM_LIMIT),
        name="ffn",
    )(tile_ea, tile_eb, tile_valid, xs_rows, wgu, wgu, wd, wd)


def _ret_kernel(pos_ref, x1_ref, mod_ref, fg_ref, y_hbm, o_ref, buf, sem, *, final_norm):
    tr = RET_TOKENS
    base = (pl.program_id(0) * pl.num_programs(1) + pl.program_id(1)) * tr

    def issue(r, carry):
        _row_copy(y_hbm, pos_ref[base + r], buf, r, sem).start()
        return carry

    lax.fori_loop(0, tr, issue, 0, unroll=8)
    pltpu.make_async_copy(y_hbm.at[pl.ds(0, tr * SUBLANES)], buf, sem).wait()

    y = jnp.concatenate([buf[pl.ds(j, tr, stride=SUBLANES), :] for j in range(SUBLANES)], axis=1)
    x2 = x1_ref[0] + mod_ref[0][5:6] * y
    if final_norm:
        ms = jnp.mean(x2 * x2, axis=-1, keepdims=True)
        x2 = x2 * lax.rsqrt(ms + EPS) * fg_ref[...]
    o_ref[0] = x2


def _ret(pos, x1, mod, final_g, y_rows, final_norm):
    batch, seq, _ = x1.shape
    tr = RET_TOKENS
    return pl.pallas_call(
        functools.partial(_ret_kernel, final_norm=final_norm),
        out_shape=jax.ShapeDtypeStruct((batch, seq, D_MODEL), F32),
        grid_spec=pltpu.PrefetchScalarGridSpec(
            num_scalar_prefetch=1,
            grid=(batch, seq // tr),
            in_specs=[
                pl.BlockSpec((1, tr, D_MODEL), lambda b, s, p: (b, s, 0)),
                pl.BlockSpec((1, 6, D_MODEL), lambda b, s, p: (b, 0, 0)),
                pl.BlockSpec((1, D_MODEL), lambda b, s, p: (0, 0)),
                pl.BlockSpec(memory_space=pl.ANY),
            ],
            out_specs=pl.BlockSpec((1, tr, D_MODEL), lambda b, s, p: (b, s, 0)),
            scratch_shapes=[pltpu.VMEM((tr * SUBLANES, LANES), F32), pltpu.SemaphoreType.DMA],
        ),
        compiler_params=pltpu.CompilerParams(
            dimension_semantics=("arbitrary", "arbitrary"), vmem_limit_bytes=VMEM_LIMIT),
        name="ret",
    )(pos, x1, mod, final_g, y_rows)


def _plan(cls, n_tiles):
    tm = FFN_ROWS
    onehot = (cls[:, None] == jnp.arange(N_CLASSES, dtype=jnp.int32)[None, :]).astype(jnp.int32)
    csum = jnp.cumsum(onehot, axis=0)
    rank = jnp.take_along_axis(csum, cls[:, None], axis=1)[:, 0] - 1
    counts = csum[-1]
    tiles = (counts + tm - 1) // tm
    tile_end = jnp.cumsum(tiles)
    tile_start = tile_end - tiles
    pos = tile_start[cls] * tm + rank
    j = jnp.arange(n_tiles, dtype=jnp.int32)
    total = tile_end[-1]
    j_eff = jnp.minimum(j, total - 1)
    tile_cls = jnp.sum((tile_end[None, :] <= j_eff[:, None]).astype(jnp.int32), axis=1)
    group = tile_cls // N_PAIRS
    pair = tile_cls % N_PAIRS
    ea = group * GROUP_SIZE + jnp.asarray(PAIR_A, jnp.int32)[pair]
    eb = group * GROUP_SIZE + jnp.asarray(PAIR_B, jnp.int32)[pair]
    return pos.astype(jnp.int32), ea, eb, (j < total).astype(jnp.int32)


def kernel(x, c, w_in, conv_w, pool_w, pool_scale, w_out, norm1_g, norm2_g, w_ada, b_ada,
           w_router, router_bias, w_gate_up, w_down, final_g):
    batch, seq, _ = x.shape
    depth = w_in.shape[0]
    tokens = batch * seq
    n_tiles = tokens // FFN_ROWS + N_CLASSES
    n_rows = n_tiles * FFN_ROWS

    mod = _ada(c, w_ada, b_ada).reshape(depth, batch, 6, D_MODEL)
    wr_t = w_router.T.astype(BF16)
    rb = router_bias.reshape(N_EXPERTS, 1).astype(F32)
    fg = final_g.reshape(1, D_MODEL)
    z = jnp.zeros((depth, POOL_GROUP, POOL_GROUP), F32)
    pw_blocks = jnp.stack([
        jnp.concatenate([jnp.concatenate([pool_w[:, 2 * k], z], axis=2),
                         jnp.concatenate([z, pool_w[:, 2 * k + 1]], axis=2)], axis=1)
        for k in range(2)], axis=1).astype(BF16)

    for l in range(depth):
        x1, h_rows, info = _mix(
            x, mod[l], norm1_g[l].reshape(1, -1), norm2_g[l].reshape(1, -1),
            w_in[l].astype(BF16), conv_w[l], pw_blocks[l], pool_scale[l].reshape(1, -1),
            w_out[l].astype(BF16), wr_t, rb)
        pos, ea, eb, valid = _plan(info[0].astype(jnp.int32), n_tiles)
        xs_rows = _dispatch(pos, h_rows, n_rows)
        y_rows = _ffn(ea, eb, valid, xs_rows, w_gate_up[l].astype(BF16), w_down[l].astype(BF16))
        x = _ret(pos, x1, mod[l], fg, y_rows, final_norm=(l == depth - 1))
    return x
```

```python
import functools

import jax
import jax.numpy as jnp
from jax import lax
from jax.experimental import pallas as pl
from jax.experimental.pallas import tpu as pltpu

D_MODEL = 1024
CONV_WIDTH = 512
POOL_WIDTH = 512
IN_COLS = 3 * CONV_WIDTH + POOL_WIDTH
POOL_WINDOWS = (2, 4, 8, 16)
POOL_GROUP = 128
N_EXPERTS = 16
N_GROUPS = 4
GROUP_SIZE = 4
D_EXPERT = 512
EPS = 1e-6

LANES = 128
SUBLANES = 8
ROW_WORDS = SUBLANES * LANES
PACKED_SLABS = D_MODEL // 2 // LANES
INFO_SLAB = PACKED_SLABS
CONV_HALO = SUBLANES
POOL_HALO = 2 * SUBLANES

PAIR_A = (0, 0, 0, 1, 1, 2)
PAIR_B = (1, 2, 3, 2, 3, 3)
N_PAIRS = len(PAIR_A)
N_CLASSES = N_GROUPS * N_PAIRS

MIX_TOKENS = 512
FFN_ROWS = 256
RET_TOKENS = 256
DISPATCH_TOKENS = 1024
VMEM_LIMIT = 56 * 1024 * 1024

F32 = jnp.float32
BF16 = jnp.bfloat16
U32 = jnp.uint32


def _norm_mod(v, gain, shift, scale):
    ms = jnp.mean(v * v, axis=-1, keepdims=True)
    y = v * lax.rsqrt(ms + EPS) * gain
    return y * (1.0 + scale) + shift


def _ada_kernel(c_ref, w_ref, b_ref, o_ref):
    c = c_ref[...]
    c_act = (c * jax.nn.sigmoid(c)).astype(BF16)
    o_ref[0] = jnp.dot(c_act, w_ref[0].astype(BF16), preferred_element_type=F32) + b_ref[0]


def _ada(c, w_ada, b_ada):
    depth = w_ada.shape[0]
    batch = c.shape[0]
    n_col = w_ada.shape[2] // D_MODEL
    return pl.pallas_call(
        _ada_kernel,
        out_shape=jax.ShapeDtypeStruct((depth, batch, n_col * D_MODEL), F32),
        grid=(depth, n_col),
        in_specs=[
            pl.BlockSpec((batch, D_MODEL), lambda l, j: (0, 0)),
            pl.BlockSpec((1, D_MODEL, D_MODEL), lambda l, j: (l, 0, j)),
            pl.BlockSpec((1, 1, D_MODEL), lambda l, j: (l, 0, j)),
        ],
        out_specs=pl.BlockSpec((1, batch, D_MODEL), lambda l, j: (l, 0, j)),
        compiler_params=pltpu.CompilerParams(
            dimension_semantics=("arbitrary", "arbitrary"), vmem_limit_bytes=VMEM_LIMIT),
        name="ada",
    )(c, w_ada, b_ada.reshape(depth, 1, -1))


def _route(logits_t, bias):
    s = jax.nn.sigmoid(logits_t)
    sel = s + bias
    sel_r = [sel[e:e + 1] for e in range(N_EXPERTS)]
    s_r = [s[e:e + 1] for e in range(N_EXPERTS)]

    def group_score(g):
        r = sel_r[GROUP_SIZE * g:GROUP_SIZE * (g + 1)]
        best = r[PAIR_A[0]] + r[PAIR_B[0]]
        for p in range(1, N_PAIRS):
            best = jnp.maximum(best, r[PAIR_A[p]] + r[PAIR_B[p]])
        return best

    best_v = group_score(0)
    best_g = jnp.zeros_like(best_v)
    for g in range(1, N_GROUPS):
        gs = group_score(g)
        upd = gs > best_v
        best_v = jnp.where(upd, gs, best_v)
        best_g = jnp.where(upd, float(g), best_g)

    def pick(rows, i):
        out = rows[i]
        for g in range(1, N_GROUPS):
            out = jnp.where(best_g == float(g), rows[GROUP_SIZE * g + i], out)
        return out

    selg = [pick(sel_r, i) for i in range(GROUP_SIZE)]
    sg = [pick(s_r, i) for i in range(GROUP_SIZE)]
    chosen = []
    for i in range(GROUP_SIZE):
        beaten = jnp.zeros_like(best_v)
        for j in range(GROUP_SIZE):
            if j == i:
                continue
            wins = selg[j] > selg[i]
            if j < i:
                wins = wins | (selg[j] == selg[i])
            beaten = beaten + wins.astype(F32)
        chosen.append(beaten < 2.0)
    m0, m1, m2, m3 = chosen
    pair = jnp.where(m0, jnp.where(m1, 0.0, jnp.where(m2, 1.0, 2.0)),
                     jnp.where(m1, jnp.where(m2, 3.0, 4.0), 5.0))
    s_a = jnp.where(m0, sg[0], jnp.where(m1, sg[1], sg[2]))
    s_b = jnp.where(m3, sg[3], jnp.where(m2, sg[2], sg[1]))
    tot = s_a + s_b
    cls = best_g * float(N_PAIRS) + pair
    zero = jnp.zeros_like(cls)
    return jnp.concatenate([cls, s_a / tot, s_b / tot] + [zero] * (SUBLANES - 3), axis=0)


def _mix_kernel(x_ref, mod_ref, n1_ref, n2_ref, win_ref, cw_ref, pw_ref, ps_ref, wout_ref,
                wr_ref, rb_ref, x1_ref, h_ref, ri_ref, gbuf, vbuf):
    ts = x_ref.shape[1]
    st = pl.program_id(1)
    x = x_ref[0]
    mod = mod_ref[0]
    sh1, sc1, g1, sh2, sc2 = (mod[i:i + 1] for i in range(5))

    h = _norm_mod(x, n1_ref[...], sh1, sc1).astype(BF16)
    proj = jnp.dot(h, win_ref[...], preferred_element_type=F32)
    u_b = proj[:, 0:CONV_WIDTH]
    u_c = proj[:, CONV_WIDTH:2 * CONV_WIDTH]
    u_h = proj[:, 2 * CONV_WIDTH:3 * CONV_WIDTH]
    v = proj[:, 3 * CONV_WIDTH:]

    @pl.when(st == 0)
    def _():
        gbuf[0:CONV_HALO] = jnp.zeros((CONV_HALO, CONV_WIDTH), F32)
        vbuf[0:POOL_HALO] = jnp.zeros((POOL_HALO, POOL_WIDTH), F32)

    g = u_c * u_h
    gbuf[CONV_HALO:CONV_HALO + ts] = g
    vbuf[POOL_HALO:POOL_HALO + ts] = v

    cw = cw_ref[...]
    conv = (cw[0:1] * gbuf[CONV_HALO - 2:CONV_HALO - 2 + ts]
            + cw[1:2] * gbuf[CONV_HALO - 1:CONV_HALO - 1 + ts]
            + cw[2:3] * g)
    y_conv = u_b * conv

    t_pos = (st * ts + lax.broadcasted_iota(jnp.int32, (ts, 1), 0) + 1).astype(F32)
    pooled = []
    for gi, win in enumerate(POOL_WINDOWS):
        lo = gi * POOL_GROUP
        v_g = v[:, lo:lo + POOL_GROUP]
        acc = v_g
        for k in range(1, win):
            acc = acc + vbuf[POOL_HALO - k:POOL_HALO - k + ts, lo:lo + POOL_GROUP]
        inv_cnt = 1.0 / jnp.minimum(t_pos, float(win))
        pooled.append(acc * inv_cnt - v_g)
    pooled = jnp.concatenate(pooled, axis=1).astype(BF16)
    half = POOL_WIDTH // 2
    y_pool = jnp.concatenate(
        [jnp.dot(pooled[:, 0:half], pw_ref[0], preferred_element_type=F32),
         jnp.dot(pooled[:, half:], pw_ref[1], preferred_element_type=F32)], axis=1) * ps_ref[...]

    gbuf[0:CONV_HALO] = gbuf[ts:ts + CONV_HALO]
    vbuf[0:POOL_HALO] = vbuf[ts:ts + POOL_HALO]

    y_mix = jnp.concatenate([y_conv, y_pool], axis=1).astype(BF16)
    x1 = x + g1 * jnp.dot(y_mix, wout_ref[...], preferred_element_type=F32)
    x1_ref[0] = x1

    h2 = _norm_mod(x1, n2_ref[...], sh2, sc2)
    logits_t = lax.dot_general(wr_ref[...], h2.astype(BF16), (((1,), (1,)), ((), ())),
                               preferred_element_type=F32)
    info = _route(logits_t, rb_ref[...])
    ri_ref[...] = info

    packed = pltpu.pack_elementwise([h2[:, :D_MODEL // 2], h2[:, D_MODEL // 2:]], packed_dtype=BF16)
    for j in range(PACKED_SLABS):
        h_ref[pl.ds(j, ts, stride=SUBLANES), :] = packed[:, j * LANES:(j + 1) * LANES]
    info_t = jnp.concatenate([info, jnp.zeros((LANES - SUBLANES, ts), F32)], axis=0).T
    h_ref[pl.ds(INFO_SLAB, ts, stride=SUBLANES), :] = lax.bitcast_convert_type(info_t, U32)
    for j in range(INFO_SLAB + 1, SUBLANES):
        h_ref[pl.ds(j, ts, stride=SUBLANES), :] = jnp.zeros((ts, LANES), U32)


def _mix(x, mod, n1, n2, win, cw, pw, ps, wout, wr, rb):
    batch, seq, _ = x.shape
    ts = MIX_TOKENS
    n_s = seq // ts
    tokens = batch * seq
    const = lambda shape: pl.BlockSpec(shape, lambda b, s: (0,) * len(shape))
    return pl.pallas_call(
        _mix_kernel,
        out_shape=(
            jax.ShapeDtypeStruct((batch, seq, D_MODEL), F32),
            jax.ShapeDtypeStruct((tokens * SUBLANES, LANES), U32),
            jax.ShapeDtypeStruct((SUBLANES, tokens), F32),
        ),
        grid=(batch, n_s),
        in_specs=[
            pl.BlockSpec((1, ts, D_MODEL), lambda b, s: (b, s, 0)),
            pl.BlockSpec((1, 6, D_MODEL), lambda b, s: (b, 0, 0)),
            const((1, D_MODEL)), const((1, D_MODEL)),
            const((D_MODEL, IN_COLS)),
            const((3, CONV_WIDTH)),
            const((2, POOL_WIDTH // 2, POOL_WIDTH // 2)),
            const((1, POOL_WIDTH)),
            const((D_MODEL, D_MODEL)),
            const((N_EXPERTS, D_MODEL)),
            const((N_EXPERTS, 1)),
        ],
        out_specs=(
            pl.BlockSpec((1, ts, D_MODEL), lambda b, s: (b, s, 0)),
            pl.BlockSpec((ts * SUBLANES, LANES), lambda b, s: (b * n_s + s, 0)),
            pl.BlockSpec((SUBLANES, ts), lambda b, s: (0, b * n_s + s)),
        ),
        scratch_shapes=[
            pltpu.VMEM((ts + CONV_HALO, CONV_WIDTH), F32),
            pltpu.VMEM((ts + POOL_HALO, POOL_WIDTH), F32),
        ],
        compiler_params=pltpu.CompilerParams(
            dimension_semantics=("arbitrary", "arbitrary"), vmem_limit_bytes=VMEM_LIMIT),
        name="mix",
    )(x, mod, n1, n2, win, cw, pw, ps, wout, wr, rb)


def _row_copy(src, src_row, dst, dst_row, sem):
    return pltpu.make_async_copy(
        src.at[pl.ds(pl.multiple_of(src_row * SUBLANES, SUBLANES), SUBLANES)],
        dst.at[pl.ds(pl.multiple_of(dst_row * SUBLANES, SUBLANES), SUBLANES)], sem)


def _dispatch_kernel(pos_ref, h_ref, init_hbm, xs_hbm, sem):
    del init_hbm
    n = DISPATCH_TOKENS
    base = pl.program_id(0) * n

    def issue(r, carry):
        _row_copy(h_ref, r, xs_hbm, pos_ref[base + r], sem).start()
        return carry

    lax.fori_loop(0, n, issue, 0, unroll=8)
    pltpu.make_async_copy(h_ref, xs_hbm.at[pl.ds(0, n * SUBLANES)], sem).wait()


def _dispatch(pos, h_rows, n_rows):
    tokens = pos.shape[0]
    init = jnp.zeros((n_rows * SUBLANES, LANES), U32)
    return pl.pallas_call(
        _dispatch_kernel,
        out_shape=jax.ShapeDtypeStruct((n_rows * SUBLANES, LANES), U32),
        grid_spec=pltpu.PrefetchScalarGridSpec(
            num_scalar_prefetch=1,
            grid=(tokens // DISPATCH_TOKENS,),
            in_specs=[pl.BlockSpec((DISPATCH_TOKENS * SUBLANES, LANES), lambda i, p: (i, 0)),
                      pl.BlockSpec(memory_space=pl.ANY)],
            out_specs=pl.BlockSpec(memory_space=pl.ANY),
            scratch_shapes=[pltpu.SemaphoreType.DMA],
        ),
        input_output_aliases={2: 0},
        compiler_params=pltpu.CompilerParams(
            dimension_semantics=("arbitrary",), vmem_limit_bytes=VMEM_LIMIT),
        name="dispatch",
    )(pos, h_rows, init)


def _ffn_kernel(ea_ref, eb_ref, valid_ref, xs_ref, wga_ref, wgb_ref, wda_ref, wdb_ref, y_ref):
    del ea_ref, eb_ref
    tm = FFN_ROWS
    i = pl.program_id(0)

    @pl.when(valid_ref[i] != 0)
    def _():
        slabs = [xs_ref[pl.ds(j, tm, stride=SUBLANES), :] for j in range(INFO_SLAB + 1)]
        packed = jnp.concatenate(slabs[:PACKED_SLABS], axis=1)
        xg = jnp.concatenate(
            [pltpu.unpack_elementwise(packed, index=k, packed_dtype=BF16, unpacked_dtype=F32)
             for k in range(2)], axis=1).astype(BF16)
        info = lax.bitcast_convert_type(slabs[INFO_SLAB], F32)

        def act(wg_ref, w):
            gu = jnp.dot(xg, wg_ref[0], preferred_element_type=F32)
            gate = gu[:, :D_EXPERT]
            up = gu[:, D_EXPERT:]
            return (gate * jax.nn.sigmoid(gate) * up * w).astype(BF16)

        y = (jnp.dot(act(wga_ref, info[:, 1:2]), wda_ref[0], preferred_element_type=F32)
             + jnp.dot(act(wgb_ref, info[:, 2:3]), wdb_ref[0], preferred_element_type=F32))
        for j in range(SUBLANES):
            y_ref[pl.ds(j, tm, stride=SUBLANES), :] = y[:, j * LANES:(j + 1) * LANES]

    @pl.when(valid_ref[i] == 0)
    def _():
        y_ref[...] = jnp.zeros(y_ref.shape, F32)


def _ffn(tile_ea, tile_eb, tile_valid, xs_rows, wgu, wd):
    n_tiles = tile_ea.shape[0]
    tm = FFN_ROWS
    return pl.pallas_call(
        _ffn_kernel,
        out_shape=jax.ShapeDtypeStruct((n_tiles * tm * SUBLANES, LANES), F32),
        grid_spec=pltpu.PrefetchScalarGridSpec(
            num_scalar_prefetch=3,
            grid=(n_tiles,),
            in_specs=[
                pl.BlockSpec((tm * SUBLANES, LANES), lambda i, ea, eb, va: (i, 0)),
                pl.BlockSpec((1, D_MODEL, 2 * D_EXPERT), lambda i, ea, eb, va: (ea[i], 0, 0)),
                pl.BlockSpec((1, D_MODEL, 2 * D_EXPERT), lambda i, ea, eb, va: (eb[i], 0, 0)),
                pl.BlockSpec((1, D_EXPERT, D_MODEL), lambda i, ea, eb, va: (ea[i], 0, 0)),
                pl.BlockSpec((1, D_EXPERT, D_MODEL), lambda i, ea, eb, va: (eb[i], 0, 0)),
            ],
            out_specs=pl.BlockSpec((tm * SUBLANES, LANES), lambda i, ea, eb, va: (i, 0)),
        ),
        compiler_params=pltpu.CompilerParams(
            dimension_semantics=("arbitrary",), vmem_limit_bytes=VMEM_LIMIT),
        name="ffn",
    )(tile_ea, tile_eb, tile_valid, xs_rows, wgu, wgu, wd, wd)


def _ret_kernel(pos_ref, x1_ref, mod_ref, fg_ref, y_hbm, o_ref, buf, sem, *, final_norm):
    tr = RET_TOKENS
    base = (pl.program_id(0) * pl.num_programs(1) + pl.program_id(1)) * tr

    def issue(r, carry):
        _row_copy(y_hbm, pos_ref[base + r], buf, r, sem).start()
        return carry

    lax.fori_loop(0, tr, issue, 0, unroll=8)
    pltpu.make_async_copy(y_hbm.at[pl.ds(0, tr * SUBLANES)], buf, sem).wait()

    y = jnp.concatenate([buf[pl.ds(j, tr, stride=SUBLANES), :] for j in range(SUBLANES)], axis=1)
    x2 = x1_ref[0] + mod_ref[0][5:6] * y
    if final_norm:
        ms = jnp.mean(x2 * x2, axis=-1, keepdims=True)
        x2 = x2 * lax.rsqrt(ms + EPS) * fg_ref[...]
    o_ref[0] = x2


def _ret(pos, x1, mod, final_g, y_rows, final_norm):
    batch, seq, _ = x1.shape
    tr = RET_TOKENS
    return pl.pallas_call(
        functools.partial(_ret_kernel, final_norm=final_norm),
        out_shape=jax.ShapeDtypeStruct((batch, seq, D_MODEL), F32),
        grid_spec=pltpu.PrefetchScalarGridSpec(
            num_scalar_prefetch=1,
            grid=(batch, seq // tr),
            in_specs=[
                pl.BlockSpec((1, tr, D_MODEL), lambda b, s, p: (b, s, 0)),
                pl.BlockSpec((1, 6, D_MODEL), lambda b, s, p: (b, 0, 0)),
                pl.BlockSpec((1, D_MODEL), lambda b, s, p: (0, 0)),
                pl.BlockSpec(memory_space=pl.ANY),
            ],
            out_specs=pl.BlockSpec((1, tr, D_MODEL), lambda b, s, p: (b, s, 0)),
            scratch_shapes=[pltpu.VMEM((tr * SUBLANES, LANES), F32), pltpu.SemaphoreType.DMA],
        ),
        compiler_params=pltpu.CompilerParams(
            dimension_semantics=("arbitrary", "arbitrary"), vmem_limit_bytes=VMEM_LIMIT),
        name="ret",
    )(pos, x1, mod, final_g, y_rows)


def _plan(cls, n_tiles):
    tm = FFN_ROWS
    onehot = (cls[:, None] == jnp.arange(N_CLASSES, dtype=jnp.int32)[None, :]).astype(jnp.int32)
    csum = jnp.cumsum(onehot, axis=0)
    rank = jnp.take_along_axis(csum, cls[:, None], axis=1)[:, 0] - 1
    counts = csum[-1]
    tiles = (counts + tm - 1) // tm
    tile_end = jnp.cumsum(tiles)
    tile_start = tile_end - tiles
    pos = tile_start[cls] * tm + rank
    j = jnp.arange(n_tiles, dtype=jnp.int32)
    total = tile_end[-1]
    j_eff = jnp.minimum(j, total - 1)
    tile_cls = jnp.sum((tile_end[None, :] <= j_eff[:, None]).astype(jnp.int32), axis=1)
    group = tile_cls // N_PAIRS
    pair = tile_cls % N_PAIRS
    ea = group * GROUP_SIZE + jnp.asarray(PAIR_A, jnp.int32)[pair]
    eb = group * GROUP_SIZE + jnp.asarray(PAIR_B, jnp.int32)[pair]
    return pos.astype(jnp.int32), ea, eb, (j < total).astype(jnp.int32)


def kernel(x, c, w_in, conv_w, pool_w, pool_scale, w_out, norm1_g, norm2_g, w_ada, b_ada,
           w_router, router_bias, w_gate_up, w_down, final_g):
    batch, seq, _ = x.shape
    depth = w_in.shape[0]
    tokens = batch * seq
    n_tiles = tokens // FFN_ROWS + N_CLASSES
    n_rows = n_tiles * FFN_ROWS

    mod = _ada(c, w_ada, b_ada).reshape(depth, batch, 6, D_MODEL)
    wr_t = w_router.T.astype(BF16)
    rb = router_bias.reshape(N_EXPERTS, 1).astype(F32)
    fg = final_g.reshape(1, D_MODEL)
    z = jnp.zeros((depth, POOL_GROUP, POOL_GROUP), F32)
    pw_blocks = jnp.stack([
        jnp.concatenate([jnp.concatenate([pool_w[:, 2 * k], z], axis=2),
                         jnp.concatenate([z, pool_w[:, 2 * k + 1]], axis=2)], axis=1)
        for k in range(2)], axis=1).astype(BF16)

    for l in range(depth):
        x1, h_rows, info = _mix(
            x, mod[l], norm1_g[l].reshape(1, -1), norm2_g[l].reshape(1, -1),
            w_in[l].astype(BF16), conv_w[l], pw_blocks[l], pool_scale[l].reshape(1, -1),
            w_out[l].astype(BF16), wr_t, rb)
        pos, ea, eb, valid = _plan(info[0].astype(jnp.int32), n_tiles)
        xs_rows = _dispatch(pos, h_rows, n_rows)
        y_rows = _ffn(ea, eb, valid, xs_rows, w_gate_up[l].astype(BF16), w_down[l].astype(BF16))
        x = _ret(pos, x1, mod[l], fg, y_rows, final_norm=(l == depth - 1))
    return x
```

```python
import functools

import jax
import jax.numpy as jnp
from jax import lax
from jax.experimental import pallas as pl
from jax.experimental.pallas import tpu as pltpu

D_MODEL = 1024
CONV_WIDTH = 512
POOL_WIDTH = 512
IN_COLS = 3 * CONV_WIDTH + POOL_WIDTH
POOL_WINDOWS = (2, 4, 8, 16)
POOL_GROUP = 128
N_EXPERTS = 16
N_GROUPS = 4
GROUP_SIZE = 4
D_EXPERT = 512
EPS = 1e-6

LANES = 128
SUBLANES = 8
ROW_WORDS = SUBLANES * LANES
PACKED_SLABS = D_MODEL // 2 // LANES
INFO_SLAB = PACKED_SLABS
CONV_HALO = SUBLANES
POOL_HALO = 2 * SUBLANES

PAIR_A = (0, 0, 0, 1, 1, 2)
PAIR_B = (1, 2, 3, 2, 3, 3)
N_PAIRS = len(PAIR_A)
N_CLASSES = N_GROUPS * N_PAIRS

MIX_TOKENS = 512
FFN_ROWS = 256
RET_TOKENS = 256
VMEM_LIMIT = 56 * 1024 * 1024

F32 = jnp.float32
BF16 = jnp.bfloat16
U32 = jnp.uint32


def _norm_mod(v, gain, shift, scale):
    ms = jnp.mean(v * v, axis=-1, keepdims=True)
    y = v * lax.rsqrt(ms + EPS) * gain
    return y * (1.0 + scale) + shift


def _ada_kernel(c_ref, w_ref, b_ref, o_ref):
    c = c_ref[...]
    c_act = (c * jax.nn.sigmoid(c)).astype(BF16)
    o_ref[0] = jnp.dot(c_act, w_ref[0].astype(BF16), preferred_element_type=F32) + b_ref[0]


def _ada(c, w_ada, b_ada):
    depth = w_ada.shape[0]
    batch = c.shape[0]
    n_col = w_ada.shape[2] // D_MODEL
    return pl.pallas_call(
        _ada_kernel,
        out_shape=jax.ShapeDtypeStruct((depth, batch, n_col * D_MODEL), F32),
        grid=(depth, n_col),
        in_specs=[
            pl.BlockSpec((batch, D_MODEL), lambda l, j: (0, 0)),
            pl.BlockSpec((1, D_MODEL, D_MODEL), lambda l, j: (l, 0, j)),
            pl.BlockSpec((1, 1, D_MODEL), lambda l, j: (l, 0, j)),
        ],
        out_specs=pl.BlockSpec((1, batch, D_MODEL), lambda l, j: (l, 0, j)),
        compiler_params=pltpu.CompilerParams(
            dimension_semantics=("arbitrary", "arbitrary"), vmem_limit_bytes=VMEM_LIMIT),
        name="ada",
    )(c, w_ada, b_ada.reshape(depth, 1, -1))


def _route(logits_t, bias):
    s = jax.nn.sigmoid(logits_t)
    sel = s + bias
    sel_r = [sel[e:e + 1] for e in range(N_EXPERTS)]
    s_r = [s[e:e + 1] for e in range(N_EXPERTS)]

    def group_score(g):
        r = sel_r[GROUP_SIZE * g:GROUP_SIZE * (g + 1)]
        best = r[PAIR_A[0]] + r[PAIR_B[0]]
        for p in range(1, N_PAIRS):
            best = jnp.maximum(best, r[PAIR_A[p]] + r[PAIR_B[p]])
        return best

    best_v = group_score(0)
    best_g = jnp.zeros_like(best_v)
    for g in range(1, N_GROUPS):
        gs = group_score(g)
        upd = gs > best_v
        best_v = jnp.where(upd, gs, best_v)
        best_g = jnp.where(upd, float(g), best_g)

    def pick(rows, i):
        out = rows[i]
        for g in range(1, N_GROUPS):
            out = jnp.where(best_g == float(g), rows[GROUP_SIZE * g + i], out)
        return out

    selg = [pick(sel_r, i) for i in range(GROUP_SIZE)]
    sg = [pick(s_r, i) for i in range(GROUP_SIZE)]
    chosen = []
    for i in range(GROUP_SIZE):
        beaten = jnp.zeros_like(best_v)
        for j in range(GROUP_SIZE):
            if j == i:
                continue
            wins = selg[j] > selg[i]
            if j < i:
                wins = wins | (selg[j] == selg[i])
            beaten = beaten + wins.astype(F32)
        chosen.append(beaten < 2.0)
    m0, m1, m2, m3 = chosen
    pair = jnp.where(m0, jnp.where(m1, 0.0, jnp.where(m2, 1.0, 2.0)),
                     jnp.where(m1, jnp.where(m2, 3.0, 4.0), 5.0))
    s_a = jnp.where(m0, sg[0], jnp.where(m1, sg[1], sg[2]))
    s_b = jnp.where(m3, sg[3], jnp.where(m2, sg[2], sg[1]))
    tot = s_a + s_b
    cls = best_g * float(N_PAIRS) + pair
    zero = jnp.zeros_like(cls)
    return jnp.concatenate([cls, s_a / tot, s_b / tot] + [zero] * (SUBLANES - 3), axis=0)


def _mix_kernel(x_ref, mod_ref, n1_ref, n2_ref, win_ref, cw_ref, pw_ref, ps_ref, wout_ref,
                wr_ref, rb_ref, x1_ref, h_ref, ri_ref, gbuf, vbuf):
    ts = x_ref.shape[1]
    st = pl.program_id(1)
    x = x_ref[0]
    mod = mod_ref[0]
    sh1, sc1, g1, sh2, sc2 = (mod[i:i + 1] for i in range(5))

    h = _norm_mod(x, n1_ref[...], sh1, sc1).astype(BF16)
    proj = jnp.dot(h, win_ref[...], preferred_element_type=F32)
    u_b = proj[:, 0:CONV_WIDTH]
    u_c = proj[:, CONV_WIDTH:2 * CONV_WIDTH]
    u_h = proj[:, 2 * CONV_WIDTH:3 * CONV_WIDTH]
    v = proj[:, 3 * CONV_WIDTH:]

    @pl.when(st == 0)
    def _():
        gbuf[0:CONV_HALO] = jnp.zeros((CONV_HALO, CONV_WIDTH), F32)
        vbuf[0:POOL_HALO] = jnp.zeros((POOL_HALO, POOL_WIDTH), F32)

    g = u_c * u_h
    gbuf[CONV_HALO:CONV_HALO + ts] = g
    vbuf[POOL_HALO:POOL_HALO + ts] = v

    cw = cw_ref[...]
    conv = (cw[0:1] * gbuf[CONV_HALO - 2:CONV_HALO - 2 + ts]
            + cw[1:2] * gbuf[CONV_HALO - 1:CONV_HALO - 1 + ts]
            + cw[2:3] * g)
    y_conv = u_b * conv

    t_pos = (st * ts + lax.broadcasted_iota(jnp.int32, (ts, 1), 0) + 1).astype(F32)
    pooled = []
    for gi, win in enumerate(POOL_WINDOWS):
        lo = gi * POOL_GROUP
        v_g = v[:, lo:lo + POOL_GROUP]
        acc = v_g
        for k in range(1, win):
            acc = acc + vbuf[POOL_HALO - k:POOL_HALO - k + ts, lo:lo + POOL_GROUP]
        inv_cnt = 1.0 / jnp.minimum(t_pos, float(win))
        pooled.append(acc * inv_cnt - v_g)
    pooled = jnp.concatenate(pooled, axis=1).astype(BF16)
    half = POOL_WIDTH // 2
    y_pool = jnp.concatenate(
        [jnp.dot(pooled[:, 0:half], pw_ref[0], preferred_element_type=F32),
         jnp.dot(pooled[:, half:], pw_ref[1], preferred_element_type=F32)], axis=1) * ps_ref[...]

    gbuf[0:CONV_HALO] = gbuf[ts:ts + CONV_HALO]
    vbuf[0:POOL_HALO] = vbuf[ts:ts + POOL_HALO]

    y_mix = jnp.concatenate([y_conv, y_pool], axis=1).astype(BF16)
    x1 = x + g1 * jnp.dot(y_mix, wout_ref[...], preferred_element_type=F32)
    x1_ref[0] = x1

    h2 = _norm_mod(x1, n2_ref[...], sh2, sc2)
    logits_t = lax.dot_general(wr_ref[...], h2.astype(BF16), (((1,), (1,)), ((), ())),
                               preferred_element_type=F32)
    info = _route(logits_t, rb_ref[...])
    ri_ref[...] = info

    packed = pltpu.pack_elementwise([h2[:, :D_MODEL // 2], h2[:, D_MODEL // 2:]], packed_dtype=BF16)
    for j in range(PACKED_SLABS):
        h_ref[pl.ds(j, ts, stride=SUBLANES), :] = packed[:, j * LANES:(j + 1) * LANES]
    info_t = jnp.concatenate([info, jnp.zeros((LANES - SUBLANES, ts), F32)], axis=0).T
    h_ref[pl.ds(INFO_SLAB, ts, stride=SUBLANES), :] = lax.bitcast_convert_type(info_t, U32)
    for j in range(INFO_SLAB + 1, SUBLANES):
        h_ref[pl.ds(j, ts, stride=SUBLANES), :] = jnp.zeros((ts, LANES), U32)


def _mix(x, mod, n1, n2, win, cw, pw, ps, wout, wr, rb):
    batch, seq, _ = x.shape
    ts = MIX_TOKENS
    n_s = seq // ts
    tokens = batch * seq
    const = lambda shape: pl.BlockSpec(shape, lambda b, s: (0,) * len(shape))
    return pl.pallas_call(
        _mix_kernel,
        out_shape=(
            jax.ShapeDtypeStruct((batch, seq, D_MODEL), F32),
            jax.ShapeDtypeStruct((tokens * SUBLANES, LANES), U32),
            jax.ShapeDtypeStruct((SUBLANES, tokens), F32),
        ),
        grid=(batch, n_s),
        in_specs=[
            pl.BlockSpec((1, ts, D_MODEL), lambda b, s: (b, s, 0)),
            pl.BlockSpec((1, 6, D_MODEL), lambda b, s: (b, 0, 0)),
            const((1, D_MODEL)), const((1, D_MODEL)),
            const((D_MODEL, IN_COLS)),
            const((3, CONV_WIDTH)),
            const((2, POOL_WIDTH // 2, POOL_WIDTH // 2)),
            const((1, POOL_WIDTH)),
            const((D_MODEL, D_MODEL)),
            const((N_EXPERTS, D_MODEL)),
            const((N_EXPERTS, 1)),
        ],
        out_specs=(
            pl.BlockSpec((1, ts, D_MODEL), lambda b, s: (b, s, 0)),
            pl.BlockSpec((ts * SUBLANES, LANES), lambda b, s: (b * n_s + s, 0)),
            pl.BlockSpec((SUBLANES, ts), lambda b, s: (0, b * n_s + s)),
        ),
        scratch_shapes=[
            pltpu.VMEM((ts + CONV_HALO, CONV_WIDTH), F32),
            pltpu.VMEM((ts + POOL_HALO, POOL_WIDTH), F32),
        ],
        compiler_params=pltpu.CompilerParams(
            dimension_semantics=("arbitrary", "arbitrary"), vmem_limit_bytes=VMEM_LIMIT),
        name="mix",
    )(x, mod, n1, n2, win, cw, pw, ps, wout, wr, rb)


def _row_copy(src, src_row, dst, dst_row, sem):
    return pltpu.make_async_copy(
        src.at[pl.ds(pl.multiple_of(src_row * SUBLANES, SUBLANES), SUBLANES)],
        dst.at[pl.ds(pl.multiple_of(dst_row * SUBLANES, SUBLANES), SUBLANES)], sem)


def _ffn_kernel(ea_ref, eb_ref, inv_ref, h_hbm, wga_ref, wgb_ref, wda_ref, wdb_ref, y_ref,
                xbuf, sems):
    del ea_ref, eb_ref
    tm = FFN_ROWS
    i = pl.program_id(0)
    n = pl.num_programs(0)
    slot = lax.rem(i, 2)

    def gather(tile, into, part=0, parts=1):
        base = tile * tm
        for r in range(part * tm // parts, (part + 1) * tm // parts):
            _row_copy(h_hbm, inv_ref[base + r], xbuf.at[into], r, sems.at[into]).start()

    def compute(prefetch):
        pltpu.make_async_copy(h_hbm.at[pl.ds(0, tm * SUBLANES)], xbuf.at[slot], sems.at[slot]).wait()
        rows = xbuf.at[slot]
        slabs = [rows[pl.ds(j, tm, stride=SUBLANES), :] for j in range(INFO_SLAB + 1)]
        packed = jnp.concatenate(slabs[:PACKED_SLABS], axis=1)
        xg = jnp.concatenate(
            [pltpu.unpack_elementwise(packed, index=k, packed_dtype=BF16, unpacked_dtype=F32)
             for k in range(2)], axis=1).astype(BF16)
        info = lax.bitcast_convert_type(slabs[INFO_SLAB], F32)
        prefetch(0)

        def act(wg_ref, w):
            gu = jnp.dot(xg, wg_ref[0], preferred_element_type=F32)
            gate = gu[:, :D_EXPERT]
            up = gu[:, D_EXPERT:]
            return (gate * jax.nn.sigmoid(gate) * up * w).astype(BF16)

        act_a = act(wga_ref, info[:, 1:2])
        prefetch(1)
        act_b = act(wgb_ref, info[:, 2:3])
        prefetch(2)
        y = (jnp.dot(act_a, wda_ref[0], preferred_element_type=F32)
             + jnp.dot(act_b, wdb_ref[0], preferred_element_type=F32))
        prefetch(3)
        for j in range(SUBLANES):
            y_ref[pl.ds(j, tm, stride=SUBLANES), :] = y[:, j * LANES:(j + 1) * LANES]

    @pl.when(i == 0)
    def _():
        gather(0, 0)

    @pl.when(i + 1 < n)
    def _():
        compute(lambda part: gather(i + 1, 1 - slot, part, 4))

    @pl.when(i + 1 == n)
    def _():
        compute(lambda part: None)


def _ffn(tile_ea, tile_eb, inv, h_rows, wgu, wd):
    n_tiles = tile_ea.shape[0]
    tm = FFN_ROWS
    return pl.pallas_call(
        _ffn_kernel,
        out_shape=jax.ShapeDtypeStruct((n_tiles * tm * SUBLANES, LANES), F32),
        grid_spec=pltpu.PrefetchScalarGridSpec(
            num_scalar_prefetch=3,
            grid=(n_tiles,),
            in_specs=[
                pl.BlockSpec(memory_space=pl.ANY),
                pl.BlockSpec((1, D_MODEL, 2 * D_EXPERT), lambda i, ea, eb, iv: (ea[i], 0, 0)),
                pl.BlockSpec((1, D_MODEL, 2 * D_EXPERT), lambda i, ea, eb, iv: (eb[i], 0, 0)),
                pl.BlockSpec((1, D_EXPERT, D_MODEL), lambda i, ea, eb, iv: (ea[i], 0, 0)),
                pl.BlockSpec((1, D_EXPERT, D_MODEL), lambda i, ea, eb, iv: (eb[i], 0, 0)),
            ],
            out_specs=pl.BlockSpec((tm * SUBLANES, LANES), lambda i, ea, eb, iv: (i, 0)),
            scratch_shapes=[pltpu.VMEM((2, tm * SUBLANES, LANES), U32),
                            pltpu.SemaphoreType.DMA((2,))],
        ),
        compiler_params=pltpu.CompilerParams(
            dimension_semantics=("arbitrary",), vmem_limit_bytes=VMEM_LIMIT),
        name="ffn",
    )(tile_ea, tile_eb, inv, h_rows, wgu, wgu, wd, wd)


def _ret_kernel(pos_ref, x1_ref, mod_ref, fg_ref, y_hbm, o_ref, buf, sem, *, final_norm):
    tr = RET_TOKENS
    base = (pl.program_id(0) * pl.num_programs(1) + pl.program_id(1)) * tr

    def issue(r, carry):
        _row_copy(y_hbm, pos_ref[base + r], buf, r, sem).start()
        return carry

    lax.fori_loop(0, tr, issue, 0, unroll=8)
    pltpu.make_async_copy(y_hbm.at[pl.ds(0, tr * SUBLANES)], buf, sem).wait()

    y = jnp.concatenate([buf[pl.ds(j, tr, stride=SUBLANES), :] for j in range(SUBLANES)], axis=1)
    x2 = x1_ref[0] + mod_ref[0][5:6] * y
    if final_norm:
        ms = jnp.mean(x2 * x2, axis=-1, keepdims=True)
        x2 = x2 * lax.rsqrt(ms + EPS) * fg_ref[...]
    o_ref[0] = x2


def _ret(pos, x1, mod, final_g, y_rows, final_norm):
    batch, seq, _ = x1.shape
    tr = RET_TOKENS
    return pl.pallas_call(
        functools.partial(_ret_kernel, final_norm=final_norm),
        out_shape=jax.ShapeDtypeStruct((batch, seq, D_MODEL), F32),
        grid_spec=pltpu.PrefetchScalarGridSpec(
            num_scalar_prefetch=1,
            grid=(batch, seq // tr),
            in_specs=[
                pl.BlockSpec((1, tr, D_MODEL), lambda b, s, p: (b, s, 0)),
                pl.BlockSpec((1, 6, D_MODEL), lambda b, s, p: (b, 0, 0)),
                pl.BlockSpec((1, D_MODEL), lambda b, s, p: (0, 0)),
                pl.BlockSpec(memory_space=pl.ANY),
            ],
            out_specs=pl.BlockSpec((1, tr, D_MODEL), lambda b, s, p: (b, s, 0)),
            scratch_shapes=[pltpu.VMEM((tr * SUBLANES, LANES), F32), pltpu.SemaphoreType.DMA],
        ),
        compiler_params=pltpu.CompilerParams(
            dimension_semantics=("arbitrary", "arbitrary"), vmem_limit_bytes=VMEM_LIMIT),
        name="ret",
    )(pos, x1, mod, final_g, y_rows)


def _plan(cls, n_tiles):
    tm = FFN_ROWS
    onehot = (cls[:, None] == jnp.arange(N_CLASSES, dtype=jnp.int32)[None, :]).astype(jnp.int32)
    csum = jnp.cumsum(onehot, axis=0)
    rank = jnp.take_along_axis(csum, cls[:, None], axis=1)[:, 0] - 1
    counts = csum[-1]
    tiles = (counts + tm - 1) // tm
    tile_end = jnp.cumsum(tiles)
    tile_start = tile_end - tiles
    pos = tile_start[cls] * tm + rank
    j = jnp.arange(n_tiles, dtype=jnp.int32)
    total = tile_end[-1]
    j_eff = jnp.minimum(j, total - 1)
    tile_cls = jnp.sum((tile_end[None, :] <= j_eff[:, None]).astype(jnp.int32), axis=1)
    group = tile_cls // N_PAIRS
    pair = tile_cls % N_PAIRS
    ea = group * GROUP_SIZE + jnp.asarray(PAIR_A, jnp.int32)[pair]
    eb = group * GROUP_SIZE + jnp.asarray(PAIR_B, jnp.int32)[pair]
    pos = pos.astype(jnp.int32)
    inv = jnp.zeros((n_tiles * tm,), jnp.int32).at[pos].set(
        jnp.arange(cls.shape[0], dtype=jnp.int32), unique_indices=True)
    return pos, inv, ea, eb


def kernel(x, c, w_in, conv_w, pool_w, pool_scale, w_out, norm1_g, norm2_g, w_ada, b_ada,
           w_router, router_bias, w_gate_up, w_down, final_g):
    batch, seq, _ = x.shape
    depth = w_in.shape[0]
    tokens = batch * seq
    n_tiles = tokens // FFN_ROWS + N_CLASSES

    mod = _ada(c, w_ada, b_ada).reshape(depth, batch, 6, D_MODEL)
    wr_t = w_router.T.astype(BF16)
    rb = router_bias.reshape(N_EXPERTS, 1).astype(F32)
    fg = final_g.reshape(1, D_MODEL)
    z = jnp.zeros((depth, POOL_GROUP, POOL_GROUP), F32)
    pw_blocks = jnp.stack([
        jnp.concatenate([jnp.concatenate([pool_w[:, 2 * k], z], axis=2),
                         jnp.concatenate([z, pool_w[:, 2 * k + 1]], axis=2)], axis=1)
        for k in range(2)], axis=1).astype(BF16)

    for l in range(depth):
        x1, h_rows, info = _mix(
            x, mod[l], norm1_g[l].reshape(1, -1), norm2_g[l].reshape(1, -1),
            w_in[l].astype(BF16), conv_w[l], pw_blocks[l], pool_scale[l].reshape(1, -1),
            w_out[l].astype(BF16), wr_t, rb)
        pos, inv, ea, eb = _plan(info[0].astype(jnp.int32), n_tiles)
        y_rows = _ffn(ea, eb, inv, h_rows, w_gate_up[l].astype(BF16), w_down[l].astype(BF16))
        x = _ret(pos, x1, mod[l], fg, y_rows, final_norm=(l == depth - 1))
    return x
```

```python
import functools

import jax
import jax.numpy as jnp
from jax import lax
from jax.experimental import pallas as pl
from jax.experimental.pallas import tpu as pltpu

D_MODEL = 1024
CONV_WIDTH = 512
POOL_WIDTH = 512
IN_COLS = 3 * CONV_WIDTH + POOL_WIDTH
POOL_WINDOWS = (2, 4, 8, 16)
POOL_GROUP = 128
N_EXPERTS = 16
N_GROUPS = 4
GROUP_SIZE = 4
D_EXPERT = 512
EPS = 1e-6

LANES = 128
SUBLANES = 8
ROW_WORDS = SUBLANES * LANES
PACKED_SLABS = D_MODEL // 2 // LANES
INFO_SLAB = PACKED_SLABS
CONV_HALO = SUBLANES
POOL_HALO = 2 * SUBLANES

PAIR_A = (0, 0, 0, 1, 1, 2)
PAIR_B = (1, 2, 3, 2, 3, 3)
N_PAIRS = len(PAIR_A)
N_CLASSES = N_GROUPS * N_PAIRS

MIX_TOKENS = 512
FFN_ROWS = 256
FIN_TOKENS = 512
GATHER_SLOTS = 3
SCATTER_SLOTS = 3
DMA_GROUPS = 4
LEAD_TILES = 3
TAIL_TILES = 2
VMEM_LIMIT = 56 * 1024 * 1024

F32 = jnp.float32
BF16 = jnp.bfloat16
U32 = jnp.uint32


def _norm_mod(v, gain, shift, scale):
    ms = jnp.mean(v * v, axis=-1, keepdims=True)
    y = v * lax.rsqrt(ms + EPS) * gain
    return y * (1.0 + scale) + shift


def _ada_kernel(c_ref, w_ref, b_ref, o_ref):
    c = c_ref[...]
    c_act = (c * jax.nn.sigmoid(c)).astype(BF16)
    o_ref[0] = jnp.dot(c_act, w_ref[0].astype(BF16), preferred_element_type=F32) + b_ref[0]


def _ada(c, w_ada, b_ada):
    depth = w_ada.shape[0]
    batch = c.shape[0]
    n_col = w_ada.shape[2] // D_MODEL
    return pl.pallas_call(
        _ada_kernel,
        out_shape=jax.ShapeDtypeStruct((depth, batch, n_col * D_MODEL), F32),
        grid=(depth, n_col),
        in_specs=[
            pl.BlockSpec((batch, D_MODEL), lambda l, j: (0, 0)),
            pl.BlockSpec((1, D_MODEL, D_MODEL), lambda l, j: (l, 0, j)),
            pl.BlockSpec((1, 1, D_MODEL), lambda l, j: (l, 0, j)),
        ],
        out_specs=pl.BlockSpec((1, batch, D_MODEL), lambda l, j: (l, 0, j)),
        compiler_params=pltpu.CompilerParams(
            dimension_semantics=("arbitrary", "arbitrary"), vmem_limit_bytes=VMEM_LIMIT),
        name="ada",
    )(c, w_ada, b_ada.reshape(depth, 1, -1))


def _route(logits_t, bias):
    s = jax.nn.sigmoid(logits_t)
    sel = s + bias
    sel_r = [sel[e:e + 1] for e in range(N_EXPERTS)]
    s_r = [s[e:e + 1] for e in range(N_EXPERTS)]

    def group_score(g):
        r = sel_r[GROUP_SIZE * g:GROUP_SIZE * (g + 1)]
        best = r[PAIR_A[0]] + r[PAIR_B[0]]
        for p in range(1, N_PAIRS):
            best = jnp.maximum(best, r[PAIR_A[p]] + r[PAIR_B[p]])
        return best

    best_v = group_score(0)
    best_g = jnp.zeros_like(best_v)
    for g in range(1, N_GROUPS):
        gs = group_score(g)
        upd = gs > best_v
        best_v = jnp.where(upd, gs, best_v)
        best_g = jnp.where(upd, float(g), best_g)

    def pick(rows, i):
        out = rows[i]
        for g in range(1, N_GROUPS):
            out = jnp.where(best_g == float(g), rows[GROUP_SIZE * g + i], out)
        return out

    selg = [pick(sel_r, i) for i in range(GROUP_SIZE)]
    sg = [pick(s_r, i) for i in range(GROUP_SIZE)]
    chosen = []
    for i in range(GROUP_SIZE):
        beaten = jnp.zeros_like(best_v)
        for j in range(GROUP_SIZE):
            if j == i:
                continue
            wins = selg[j] > selg[i]
            if j < i:
                wins = wins | (selg[j] == selg[i])
            beaten = beaten + wins.astype(F32)
        chosen.append(beaten < 2.0)
    m0, m1, m2, m3 = chosen
    pair = jnp.where(m0, jnp.where(m1, 0.0, jnp.where(m2, 1.0, 2.0)),
                     jnp.where(m1, jnp.where(m2, 3.0, 4.0), 5.0))
    s_a = jnp.where(m0, sg[0], jnp.where(m1, sg[1], sg[2]))
    s_b = jnp.where(m3, sg[3], jnp.where(m2, sg[2], sg[1]))
    tot = s_a + s_b
    cls = best_g * float(N_PAIRS) + pair
    zero = jnp.zeros_like(cls)
    return jnp.concatenate([cls, s_a / tot, s_b / tot] + [zero] * (SUBLANES - 3), axis=0)


def _rows_to_tokens(rows_ref, n):
    return jnp.concatenate(
        [rows_ref[pl.ds(j, n, stride=SUBLANES), :] for j in range(SUBLANES)], axis=1)


def _mix_kernel(*refs, prev_moe):
    if prev_moe:
        x_ref, y_ref, pmod_ref, *refs = refs
    else:
        x_ref, *refs = refs
    (mod_ref, n1_ref, n2_ref, win_ref, cw_ref, pw_ref, ps_ref, wout_ref, wr_ref, rb_ref,
     x1_ref, h_ref, ri_ref, gbuf, vbuf) = refs
    ts = x_ref.shape[1]
    st = pl.program_id(1)
    x = x_ref[0]
    if prev_moe:
        x = x + pmod_ref[0][5:6] * _rows_to_tokens(y_ref, ts)
    mod = mod_ref[0]
    sh1, sc1, g1, sh2, sc2 = (mod[i:i + 1] for i in range(5))

    h = _norm_mod(x, n1_ref[...], sh1, sc1).astype(BF16)
    proj = jnp.dot(h, win_ref[...], preferred_element_type=F32)
    u_b = proj[:, 0:CONV_WIDTH]
    u_c = proj[:, CONV_WIDTH:2 * CONV_WIDTH]
    u_h = proj[:, 2 * CONV_WIDTH:3 * CONV_WIDTH]
    v = proj[:, 3 * CONV_WIDTH:]

    @pl.when(st == 0)
    def _():
        gbuf[0:CONV_HALO] = jnp.zeros((CONV_HALO, CONV_WIDTH), F32)
        vbuf[0:POOL_HALO] = jnp.zeros((POOL_HALO, POOL_WIDTH), F32)

    g = u_c * u_h
    gbuf[CONV_HALO:CONV_HALO + ts] = g
    vbuf[POOL_HALO:POOL_HALO + ts] = v

    cw = cw_ref[...]
    conv = (cw[0:1] * gbuf[CONV_HALO - 2:CONV_HALO - 2 + ts]
            + cw[1:2] * gbuf[CONV_HALO - 1:CONV_HALO - 1 + ts]
            + cw[2:3] * g)
    y_conv = u_b * conv

    t_pos = (st * ts + lax.broadcasted_iota(jnp.int32, (ts, 1), 0) + 1).astype(F32)
    pooled = []
    for gi, win in enumerate(POOL_WINDOWS):
        lo = gi * POOL_GROUP
        v_g = v[:, lo:lo + POOL_GROUP]
        acc = v_g
        for k in range(1, win):
            acc = acc + vbuf[POOL_HALO - k:POOL_HALO - k + ts, lo:lo + POOL_GROUP]
        inv_cnt = 1.0 / jnp.minimum(t_pos, float(win))
        pooled.append(acc * inv_cnt - v_g)
    pooled = jnp.concatenate(pooled, axis=1).astype(BF16)
    half = POOL_WIDTH // 2
    y_pool = jnp.concatenate(
        [jnp.dot(pooled[:, 0:half], pw_ref[0], preferred_element_type=F32),
         jnp.dot(pooled[:, half:], pw_ref[1], preferred_element_type=F32)], axis=1) * ps_ref[...]

    gbuf[0:CONV_HALO] = gbuf[ts:ts + CONV_HALO]
    vbuf[0:POOL_HALO] = vbuf[ts:ts + POOL_HALO]

    y_mix = jnp.concatenate([y_conv, y_pool], axis=1).astype(BF16)
    x1 = x + g1 * jnp.dot(y_mix, wout_ref[...], preferred_element_type=F32)
    x1_ref[0] = x1

    h2 = _norm_mod(x1, n2_ref[...], sh2, sc2)
    logits_t = lax.dot_general(wr_ref[...], h2.astype(BF16), (((1,), (1,)), ((), ())),
                               preferred_element_type=F32)
    info = _route(logits_t, rb_ref[...])
    ri_ref[...] = info

    packed = pltpu.pack_elementwise([h2[:, :D_MODEL // 2], h2[:, D_MODEL // 2:]], packed_dtype=BF16)
    for j in range(PACKED_SLABS):
        h_ref[pl.ds(j, ts, stride=SUBLANES), :] = packed[:, j * LANES:(j + 1) * LANES]
    info_t = jnp.concatenate([info, jnp.zeros((LANES - SUBLANES, ts), F32)], axis=0).T
    h_ref[pl.ds(INFO_SLAB, ts, stride=SUBLANES), :] = lax.bitcast_convert_type(info_t, U32)
    for j in range(INFO_SLAB + 1, SUBLANES):
        h_ref[pl.ds(j, ts, stride=SUBLANES), :] = jnp.zeros((ts, LANES), U32)


def _mix(x, prev, mod, n1, n2, win, cw, pw, ps, wout, wr, rb):
    batch, seq, _ = x.shape
    ts = MIX_TOKENS
    n_s = seq // ts
    tokens = batch * seq
    const = lambda shape: pl.BlockSpec(shape, lambda b, s: (0,) * len(shape))
    prev_specs = [] if prev is None else [
        pl.BlockSpec((ts * SUBLANES, LANES), lambda b, s: (b * n_s + s, 0)),
        pl.BlockSpec((1, 6, D_MODEL), lambda b, s: (b, 0, 0))]
    return pl.pallas_call(
        functools.partial(_mix_kernel, prev_moe=prev is not None),
        out_shape=(
            jax.ShapeDtypeStruct((batch, seq, D_MODEL), F32),
            jax.ShapeDtypeStruct((tokens * SUBLANES, LANES), U32),
            jax.ShapeDtypeStruct((SUBLANES, tokens), F32),
        ),
        grid=(batch, n_s),
        in_specs=[
            pl.BlockSpec((1, ts, D_MODEL), lambda b, s: (b, s, 0)),
            *prev_specs,
            pl.BlockSpec((1, 6, D_MODEL), lambda b, s: (b, 0, 0)),
            const((1, D_MODEL)), const((1, D_MODEL)),
            const((D_MODEL, IN_COLS)),
            const((3, CONV_WIDTH)),
            const((2, POOL_WIDTH // 2, POOL_WIDTH // 2)),
            const((1, POOL_WIDTH)),
            const((D_MODEL, D_MODEL)),
            const((N_EXPERTS, D_MODEL)),
            const((N_EXPERTS, 1)),
        ],
        out_specs=(
            pl.BlockSpec((1, ts, D_MODEL), lambda b, s: (b, s, 0)),
            pl.BlockSpec((ts * SUBLANES, LANES), lambda b, s: (b * n_s + s, 0)),
            pl.BlockSpec((SUBLANES, ts), lambda b, s: (0, b * n_s + s)),
        ),
        scratch_shapes=[
            pltpu.VMEM((ts + CONV_HALO, CONV_WIDTH), F32),
            pltpu.VMEM((ts + POOL_HALO, POOL_WIDTH), F32),
        ],
        compiler_params=pltpu.CompilerParams(
            dimension_semantics=("arbitrary", "arbitrary"), vmem_limit_bytes=VMEM_LIMIT),
        name="mix",
    )(x, *(() if prev is None else prev), mod, n1, n2, win, cw, pw, ps, wout, wr, rb)


def _row_copy(src, src_row, dst, dst_row, sem):
    return pltpu.make_async_copy(
        src.at[pl.ds(pl.multiple_of(src_row * SUBLANES, SUBLANES), SUBLANES)],
        dst.at[pl.ds(pl.multiple_of(dst_row * SUBLANES, SUBLANES), SUBLANES)], sem)


def _ffn_kernel(ea_ref, eb_ref, dst_ref, h_hbm, wga_ref, wgb_ref, wda_ref, wdb_ref, y_hbm,
                xbuf, ybuf, gsem, ssem, *, tokens):
    del ea_ref, eb_ref
    tm = FFN_ROWS
    i = pl.program_id(0)
    n = pl.num_programs(0)

    def dst_row(tile, r):
        return dst_ref[(tile + LEAD_TILES) * tm + r]

    def part_rows(part, parts):
        return range(part * tm // parts, (part + 1) * tm // parts)

    def gather(tile, part=0, parts=1):
        slot = lax.rem(tile, GATHER_SLOTS)
        for r in part_rows(part, parts):
            src = jnp.minimum(dst_row(tile, r), tokens - 1)
            _row_copy(h_hbm, src, xbuf.at[slot], r, gsem.at[slot]).start()

    def scatter(tile, part=0, parts=1):
        slot = lax.rem(tile + LEAD_TILES, SCATTER_SLOTS)
        for r in part_rows(part, parts):
            _row_copy(ybuf.at[slot], r, y_hbm, dst_row(tile, r), ssem.at[slot]).start()

    def wait_gather(tile):
        slot = lax.rem(tile, GATHER_SLOTS)
        pltpu.make_async_copy(h_hbm.at[pl.ds(0, tm * SUBLANES)],
                              xbuf.at[slot, pl.ds(0, tm * SUBLANES)], gsem.at[slot]).wait()

    def wait_scatter(tile):
        slot = lax.rem(tile + LEAD_TILES, SCATTER_SLOTS)
        pltpu.make_async_copy(ybuf.at[slot], y_hbm.at[pl.ds(0, tm * SUBLANES)], ssem.at[slot]).wait()

    @pl.when(i == 0)
    def _():
        ybuf[...] = jnp.zeros(ybuf.shape, F32)
        xbuf[:, tm * SUBLANES:, :] = jnp.zeros((GATHER_SLOTS, SUBLANES, LANES), U32)
        gather(0)
        gather(1)
        scatter(-3)
        scatter(-2)

    def move_rows(part):
        scatter(i - 1, part, DMA_GROUPS)
        gather(i + 2, part, DMA_GROUPS)
        spare = xbuf[lax.rem(i + 2 + GATHER_SLOTS, GATHER_SLOTS), tm * SUBLANES:, :]
        return ((spare >> 16) >> 16)[0:1]

    wait_scatter(i - 3)
    wait_gather(i)
    rows = xbuf.at[lax.rem(i, GATHER_SLOTS)]
    slabs = [rows[pl.ds(j, tm, stride=SUBLANES), :] for j in range(INFO_SLAB + 1)]
    packed = jnp.concatenate(slabs[:PACKED_SLABS], axis=1)
    xg = jnp.concatenate(
        [pltpu.unpack_elementwise(packed, index=k, packed_dtype=BF16, unpacked_dtype=F32)
         for k in range(2)], axis=1).astype(BF16)
    info_a = lax.bitcast_convert_type(slabs[INFO_SLAB] | move_rows(0), F32)

    def act(wg_ref, w):
        gu = jnp.dot(xg, wg_ref[0], preferred_element_type=F32)
        gate = gu[:, :D_EXPERT]
        up = gu[:, D_EXPERT:]
        return (gate * jax.nn.sigmoid(gate) * up * w).astype(BF16)

    act_a = act(wga_ref, info_a[:, 1:2])
    info_b = lax.bitcast_convert_type(slabs[INFO_SLAB] | move_rows(1), F32)
    act_b = act(wgb_ref, info_b[:, 2:3])
    zero_lo = lax.bitcast_convert_type(move_rows(2), F32)
    y = (jnp.dot(act_a, wda_ref[0], preferred_element_type=F32)
         + jnp.dot(act_b, wdb_ref[0], preferred_element_type=F32))
    zero_hi = lax.bitcast_convert_type(move_rows(3), F32)
    out = ybuf.at[lax.rem(i, SCATTER_SLOTS)]
    for j in range(SUBLANES):
        zero = zero_lo if j < SUBLANES // 2 else zero_hi
        out[pl.ds(j, tm, stride=SUBLANES), :] = y[:, j * LANES:(j + 1) * LANES] + zero

    @pl.when(i == n - 1)
    def _():
        scatter(i)
        wait_scatter(i - 2)
        wait_scatter(i - 1)
        wait_scatter(i)
        wait_gather(i + 1)
        wait_gather(i + 2)


def _ffn(tile_ea, tile_eb, dst, h_rows, wgu, wd):
    n_tiles = tile_ea.shape[0]
    tm = FFN_ROWS
    tokens = h_rows.shape[0] // SUBLANES
    return pl.pallas_call(
        functools.partial(_ffn_kernel, tokens=tokens),
        out_shape=jax.ShapeDtypeStruct(
            ((tokens + SCATTER_SLOTS * tm) * SUBLANES, LANES), F32),
        grid_spec=pltpu.PrefetchScalarGridSpec(
            num_scalar_prefetch=3,
            grid=(n_tiles,),
            in_specs=[
                pl.BlockSpec(memory_space=pl.ANY),
                pl.BlockSpec((1, D_MODEL, 2 * D_EXPERT), lambda i, ea, eb, ds: (ea[i], 0, 0)),
                pl.BlockSpec((1, D_MODEL, 2 * D_EXPERT), lambda i, ea, eb, ds: (eb[i], 0, 0)),
                pl.BlockSpec((1, D_EXPERT, D_MODEL), lambda i, ea, eb, ds: (ea[i], 0, 0)),
                pl.BlockSpec((1, D_EXPERT, D_MODEL), lambda i, ea, eb, ds: (eb[i], 0, 0)),
            ],
            out_specs=pl.BlockSpec(memory_space=pl.ANY),
            scratch_shapes=[pltpu.VMEM((GATHER_SLOTS, (tm + 1) * SUBLANES, LANES), U32),
                            pltpu.VMEM((SCATTER_SLOTS, tm * SUBLANES, LANES), F32),
                            pltpu.SemaphoreType.DMA((GATHER_SLOTS,)),
                            pltpu.SemaphoreType.DMA((SCATTER_SLOTS,))],
        ),
        compiler_params=pltpu.CompilerParams(
            dimension_semantics=("arbitrary",), vmem_limit_bytes=VMEM_LIMIT),
        name="ffn",
    )(tile_ea, tile_eb, dst, h_rows, wgu, wgu, wd, wd)


def _fin_kernel(x1_ref, y_ref, mod_ref, fg_ref, o_ref):
    x2 = x1_ref[0] + mod_ref[0][5:6] * _rows_to_tokens(y_ref, x1_ref.shape[1])
    ms = jnp.mean(x2 * x2, axis=-1, keepdims=True)
    o_ref[0] = x2 * lax.rsqrt(ms + EPS) * fg_ref[...]


def _fin(x1, y_rows, mod, final_g):
    batch, seq, _ = x1.shape
    tf = FIN_TOKENS
    n_s = seq // tf
    return pl.pallas_call(
        _fin_kernel,
        out_shape=jax.ShapeDtypeStruct((batch, seq, D_MODEL), F32),
        grid=(batch, n_s),
        in_specs=[
            pl.BlockSpec((1, tf, D_MODEL), lambda b, s: (b, s, 0)),
            pl.BlockSpec((tf * SUBLANES, LANES), lambda b, s: (b * n_s + s, 0)),
            pl.BlockSpec((1, 6, D_MODEL), lambda b, s: (b, 0, 0)),
            pl.BlockSpec((1, D_MODEL), lambda b, s: (0, 0)),
        ],
        out_specs=pl.BlockSpec((1, tf, D_MODEL), lambda b, s: (b, s, 0)),
        compiler_params=pltpu.CompilerParams(
            dimension_semantics=("arbitrary", "arbitrary"), vmem_limit_bytes=VMEM_LIMIT),
        name="fin",
    )(x1, y_rows, mod, final_g)


def _plan(cls, n_tiles):
    tm = FFN_ROWS
    onehot = (cls[:, None] == jnp.arange(N_CLASSES, dtype=jnp.int32)[None, :]).astype(jnp.int32)
    csum = jnp.cumsum(onehot, axis=0)
    rank = jnp.take_along_axis(csum, cls[:, None], axis=1)[:, 0] - 1
    counts = csum[-1]
    tiles = (counts + tm - 1) // tm
    tile_end = jnp.cumsum(tiles)
    tile_start = tile_end - tiles
    pos = tile_start[cls] * tm + rank
    j = jnp.arange(n_tiles, dtype=jnp.int32)
    total = tile_end[-1]
    j_eff = jnp.minimum(j, total - 1)
    tile_cls = jnp.sum((tile_end[None, :] <= j_eff[:, None]).astype(jnp.int32), axis=1)
    group = tile_cls // N_PAIRS
    pair = tile_cls % N_PAIRS
    ea = group * GROUP_SIZE + jnp.asarray(PAIR_A, jnp.int32)[pair]
    eb = group * GROUP_SIZE + jnp.asarray(PAIR_B, jnp.int32)[pair]
    tokens = cls.shape[0]
    rows = jnp.arange(-LEAD_TILES * tm, (n_tiles + TAIL_TILES) * tm, dtype=jnp.int32)
    dump = tokens + ((rows // tm) % SCATTER_SLOTS) * tm + rows % tm
    dst = dump.at[pos.astype(jnp.int32) + LEAD_TILES * tm].set(
        jnp.arange(tokens, dtype=jnp.int32), unique_indices=True)
    return dst, ea, eb


def kernel(x, c, w_in, conv_w, pool_w, pool_scale, w_out, norm1_g, norm2_g, w_ada, b_ada,
           w_router, router_bias, w_gate_up, w_down, final_g):
    batch, seq, _ = x.shape
    depth = w_in.shape[0]
    tokens = batch * seq
    n_tiles = tokens // FFN_ROWS + N_CLASSES

    mod = _ada(c, w_ada, b_ada).reshape(depth, batch, 6, D_MODEL)
    wr_t = w_router.T.astype(BF16)
    rb = router_bias.reshape(N_EXPERTS, 1).astype(F32)
    fg = final_g.reshape(1, D_MODEL)
    z = jnp.zeros((depth, POOL_GROUP, POOL_GROUP), F32)
    pw_blocks = jnp.stack([
        jnp.concatenate([jnp.concatenate([pool_w[:, 2 * k], z], axis=2),
                         jnp.concatenate([z, pool_w[:, 2 * k + 1]], axis=2)], axis=1)
        for k in range(2)], axis=1).astype(BF16)

    prev = None
    for l in range(depth):
        x, h_rows, info = _mix(
            x, prev, mod[l], norm1_g[l].reshape(1, -1), norm2_g[l].reshape(1, -1),
            w_in[l].astype(BF16), conv_w[l], pw_blocks[l], pool_scale[l].reshape(1, -1),
            w_out[l].astype(BF16), wr_t, rb)
        dst, ea, eb = _plan(info[0].astype(jnp.int32), n_tiles)
        y_rows = _ffn(ea, eb, dst, h_rows, w_gate_up[l].astype(BF16), w_down[l].astype(BF16))
        prev = (y_rows, mod[l])
    return _fin(x, y_rows, mod[depth - 1], fg)
```

```python
import functools

import jax
import jax.numpy as jnp
from jax import lax
from jax.experimental import pallas as pl
from jax.experimental.pallas import tpu as pltpu

D_MODEL = 1024
CONV_WIDTH = 512
POOL_WIDTH = 512
IN_COLS = 3 * CONV_WIDTH + POOL_WIDTH
POOL_WINDOWS = (2, 4, 8, 16)
POOL_GROUP = 128
N_EXPERTS = 16
N_GROUPS = 4
GROUP_SIZE = 4
D_EXPERT = 512
EPS = 1e-6

LANES = 128
SUBLANES = 8
ROW_WORDS = SUBLANES * LANES
PACKED_SLABS = D_MODEL // 2 // LANES
INFO_SLAB = PACKED_SLABS
CONV_HALO = SUBLANES
POOL_HALO = 2 * SUBLANES

PAIR_A = (0, 0, 0, 1, 1, 2)
PAIR_B = (1, 2, 3, 2, 3, 3)
N_PAIRS = len(PAIR_A)
N_CLASSES = N_GROUPS * N_PAIRS

MIX_TOKENS = 512
FFN_ROWS = 256
FIN_TOKENS = 512
GATHER_SLOTS = 3
SCATTER_SLOTS = 3
DMA_GROUPS = 4
DMA_QUEUES = 2
LEAD_TILES = 3
TAIL_TILES = 2
VMEM_LIMIT = 56 * 1024 * 1024

F32 = jnp.float32
BF16 = jnp.bfloat16
U32 = jnp.uint32


def _norm_mod(v, gain, shift, scale):
    ms = jnp.mean(v * v, axis=-1, keepdims=True)
    y = v * lax.rsqrt(ms + EPS) * gain
    return y * (1.0 + scale) + shift


def _ada_kernel(c_ref, w_ref, b_ref, o_ref):
    c = c_ref[...]
    c_act = (c * jax.nn.sigmoid(c)).astype(BF16)
    o_ref[0] = jnp.dot(c_act, w_ref[0].astype(BF16), preferred_element_type=F32) + b_ref[0]


def _ada(c, w_ada, b_ada):
    depth = w_ada.shape[0]
    batch = c.shape[0]
    n_col = w_ada.shape[2] // D_MODEL
    return pl.pallas_call(
        _ada_kernel,
        out_shape=jax.ShapeDtypeStruct((depth, batch, n_col * D_MODEL), F32),
        grid=(depth, n_col),
        in_specs=[
            pl.BlockSpec((batch, D_MODEL), lambda l, j: (0, 0)),
            pl.BlockSpec((1, D_MODEL, D_MODEL), lambda l, j: (l, 0, j)),
            pl.BlockSpec((1, 1, D_MODEL), lambda l, j: (l, 0, j)),
        ],
        out_specs=pl.BlockSpec((1, batch, D_MODEL), lambda l, j: (l, 0, j)),
        compiler_params=pltpu.CompilerParams(
            dimension_semantics=("arbitrary", "arbitrary"), vmem_limit_bytes=VMEM_LIMIT),
        name="ada",
    )(c, w_ada, b_ada.reshape(depth, 1, -1))


def _route(logits_t, bias):
    s = jax.nn.sigmoid(logits_t)
    sel = s + bias
    sel_r = [sel[e:e + 1] for e in range(N_EXPERTS)]
    s_r = [s[e:e + 1] for e in range(N_EXPERTS)]

    def group_score(g):
        r = sel_r[GROUP_SIZE * g:GROUP_SIZE * (g + 1)]
        best = r[PAIR_A[0]] + r[PAIR_B[0]]
        for p in range(1, N_PAIRS):
            best = jnp.maximum(best, r[PAIR_A[p]] + r[PAIR_B[p]])
        return best

    best_v = group_score(0)
    best_g = jnp.zeros_like(best_v)
    for g in range(1, N_GROUPS):
        gs = group_score(g)
        upd = gs > best_v
        best_v = jnp.where(upd, gs, best_v)
        best_g = jnp.where(upd, float(g), best_g)

    def pick(rows, i):
        out = rows[i]
        for g in range(1, N_GROUPS):
            out = jnp.where(best_g == float(g), rows[GROUP_SIZE * g + i], out)
        return out

    selg = [pick(sel_r, i) for i in range(GROUP_SIZE)]
    sg = [pick(s_r, i) for i in range(GROUP_SIZE)]
    chosen = []
    for i in range(GROUP_SIZE):
        beaten = jnp.zeros_like(best_v)
        for j in range(GROUP_SIZE):
            if j == i:
                continue
            wins = selg[j] > selg[i]
            if j < i:
                wins = wins | (selg[j] == selg[i])
            beaten = beaten + wins.astype(F32)
        chosen.append(beaten < 2.0)
    m0, m1, m2, m3 = chosen
    pair = jnp.where(m0, jnp.where(m1, 0.0, jnp.where(m2, 1.0, 2.0)),
                     jnp.where(m1, jnp.where(m2, 3.0, 4.0), 5.0))
    s_a = jnp.where(m0, sg[0], jnp.where(m1, sg[1], sg[2]))
    s_b = jnp.where(m3, sg[3], jnp.where(m2, sg[2], sg[1]))
    tot = s_a + s_b
    cls = best_g * float(N_PAIRS) + pair
    zero = jnp.zeros_like(cls)
    return jnp.concatenate([cls, s_a / tot, s_b / tot] + [zero] * (SUBLANES - 3), axis=0)


def _rows_to_tokens(rows_ref, n):
    return jnp.concatenate(
        [rows_ref[pl.ds(j, n, stride=SUBLANES), :] for j in range(SUBLANES)], axis=1)


def _mix_kernel(*refs, prev_moe):
    if prev_moe:
        x_ref, y_ref, pmod_ref, *refs = refs
    else:
        x_ref, *refs = refs
    (mod_ref, n1_ref, n2_ref, win_ref, cw_ref, pw_ref, ps_ref, wout_ref, wr_ref, rb_ref,
     x1_ref, h_ref, ri_ref, gbuf, vbuf) = refs
    ts = x_ref.shape[1]
    st = pl.program_id(1)
    x = x_ref[0]
    if prev_moe:
        x = x + pmod_ref[0][5:6] * _rows_to_tokens(y_ref, ts)
    mod = mod_ref[0]
    sh1, sc1, g1, sh2, sc2 = (mod[i:i + 1] for i in range(5))

    h = _norm_mod(x, n1_ref[...], sh1, sc1).astype(BF16)
    proj = jnp.dot(h, win_ref[...], preferred_element_type=F32)
    u_b = proj[:, 0:CONV_WIDTH]
    u_c = proj[:, CONV_WIDTH:2 * CONV_WIDTH]
    u_h = proj[:, 2 * CONV_WIDTH:3 * CONV_WIDTH]
    v = proj[:, 3 * CONV_WIDTH:]

    @pl.when(st == 0)
    def _():
        gbuf[0:CONV_HALO] = jnp.zeros((CONV_HALO, CONV_WIDTH), F32)
        vbuf[0:POOL_HALO] = jnp.zeros((POOL_HALO, POOL_WIDTH), F32)

    g = u_c * u_h
    gbuf[CONV_HALO:CONV_HALO + ts] = g
    vbuf[POOL_HALO:POOL_HALO + ts] = v

    cw = cw_ref[...]
    conv = (cw[0:1] * gbuf[CONV_HALO - 2:CONV_HALO - 2 + ts]
            + cw[1:2] * gbuf[CONV_HALO - 1:CONV_HALO - 1 + ts]
            + cw[2:3] * g)
    y_conv = u_b * conv

    t_pos = (st * ts + lax.broadcasted_iota(jnp.int32, (ts, 1), 0) + 1).astype(F32)
    pooled = []
    for gi, win in enumerate(POOL_WINDOWS):
        lo = gi * POOL_GROUP
        v_g = v[:, lo:lo + POOL_GROUP]
        acc = v_g
        for k in range(1, win):
            acc = acc + vbuf[POOL_HALO - k:POOL_HALO - k + ts, lo:lo + POOL_GROUP]
        inv_cnt = 1.0 / jnp.minimum(t_pos, float(win))
        pooled.append(acc * inv_cnt - v_g)
    pooled = jnp.concatenate(pooled, axis=1).astype(BF16)
    half = POOL_WIDTH // 2
    y_pool = jnp.concatenate(
        [jnp.dot(pooled[:, 0:half], pw_ref[0], preferred_element_type=F32),
         jnp.dot(pooled[:, half:], pw_ref[1], preferred_element_type=F32)], axis=1) * ps_ref[...]

    gbuf[0:CONV_HALO] = gbuf[ts:ts + CONV_HALO]
    vbuf[0:POOL_HALO] = vbuf[ts:ts + POOL_HALO]

    y_mix = jnp.concatenate([y_conv, y_pool], axis=1).astype(BF16)
    x1 = x + g1 * jnp.dot(y_mix, wout_ref[...], preferred_element_type=F32)
    x1_ref[0] = x1

    h2 = _norm_mod(x1, n2_ref[...], sh2, sc2)
    logits_t = lax.dot_general(wr_ref[...], h2.astype(BF16), (((1,), (1,)), ((), ())),
                               preferred_element_type=F32)
    info = _route(logits_t, rb_ref[...])
    ri_ref[...] = info

    packed = pltpu.pack_elementwise([h2[:, :D_MODEL // 2], h2[:, D_MODEL // 2:]], packed_dtype=BF16)
    for j in range(PACKED_SLABS):
        h_ref[pl.ds(j, ts, stride=SUBLANES), :] = packed[:, j * LANES:(j + 1) * LANES]
    info_t = jnp.concatenate([info, jnp.zeros((LANES - SUBLANES, ts), F32)], axis=0).T
    h_ref[pl.ds(INFO_SLAB, ts, stride=SUBLANES), :] = lax.bitcast_convert_type(info_t, U32)
    for j in range(INFO_SLAB + 1, SUBLANES):
        h_ref[pl.ds(j, ts, stride=SUBLANES), :] = jnp.zeros((ts, LANES), U32)


def _mix(x, prev, mod, n1, n2, win, cw, pw, ps, wout, wr, rb):
    batch, seq, _ = x.shape
    ts = MIX_TOKENS
    n_s = seq // ts
    tokens = batch * seq
    const = lambda shape: pl.BlockSpec(shape, lambda b, s: (0,) * len(shape))
    prev_specs = [] if prev is None else [
        pl.BlockSpec((ts * SUBLANES, LANES), lambda b, s: (b * n_s + s, 0)),
        pl.BlockSpec((1, 6, D_MODEL), lambda b, s: (b, 0, 0))]
    return pl.pallas_call(
        functools.partial(_mix_kernel, prev_moe=prev is not None),
        out_shape=(
            jax.ShapeDtypeStruct((batch, seq, D_MODEL), F32),
            jax.ShapeDtypeStruct((tokens * SUBLANES, LANES), U32),
            jax.ShapeDtypeStruct((SUBLANES, tokens), F32),
        ),
        grid=(batch, n_s),
        in_specs=[
            pl.BlockSpec((1, ts, D_MODEL), lambda b, s: (b, s, 0)),
            *prev_specs,
            pl.BlockSpec((1, 6, D_MODEL), lambda b, s: (b, 0, 0)),
            const((1, D_MODEL)), const((1, D_MODEL)),
            const((D_MODEL, IN_COLS)),
            const((3, CONV_WIDTH)),
            const((2, POOL_WIDTH // 2, POOL_WIDTH // 2)),
            const((1, POOL_WIDTH)),
            const((D_MODEL, D_MODEL)),
            const((N_EXPERTS, D_MODEL)),
            const((N_EXPERTS, 1)),
        ],
        out_specs=(
            pl.BlockSpec((1, ts, D_MODEL), lambda b, s: (b, s, 0)),
            pl.BlockSpec((ts * SUBLANES, LANES), lambda b, s: (b * n_s + s, 0)),
            pl.BlockSpec((SUBLANES, ts), lambda b, s: (0, b * n_s + s)),
        ),
        scratch_shapes=[
            pltpu.VMEM((ts + CONV_HALO, CONV_WIDTH), F32),
            pltpu.VMEM((ts + POOL_HALO, POOL_WIDTH), F32),
        ],
        compiler_params=pltpu.CompilerParams(
            dimension_semantics=("arbitrary", "arbitrary"), vmem_limit_bytes=VMEM_LIMIT),
        name="mix",
    )(x, *(() if prev is None else prev), mod, n1, n2, win, cw, pw, ps, wout, wr, rb)


def _row_copy(src, src_row, dst, dst_row, sem):
    return pltpu.make_async_copy(
        src.at[pl.ds(pl.multiple_of(src_row * SUBLANES, SUBLANES), SUBLANES)],
        dst.at[pl.ds(pl.multiple_of(dst_row * SUBLANES, SUBLANES), SUBLANES)], sem)


def _ffn_kernel(ea_ref, eb_ref, dst_ref, h_hbm, wga_ref, wgb_ref, wda_ref, wdb_ref, y_hbm,
                xbuf, ybuf, gsem, ssem, *, tokens):
    del ea_ref, eb_ref
    tm = FFN_ROWS
    i = pl.program_id(0)
    n = pl.num_programs(0)

    def dst_row(tile, r):
        return dst_ref[(tile + LEAD_TILES) * tm + r]

    def part_rows(part, parts):
        return range(part * tm // parts, (part + 1) * tm // parts)

    def gather(tile, part=0, parts=1):
        slot = lax.rem(tile, GATHER_SLOTS)
        for r in part_rows(part, parts):
            src = jnp.minimum(dst_row(tile, r), tokens - 1)
            _row_copy(h_hbm, src, xbuf.at[slot], r, gsem.at[slot]).start(priority=r % DMA_QUEUES)

    def scatter(tile, part=0, parts=1):
        slot = lax.rem(tile + LEAD_TILES, SCATTER_SLOTS)
        for r in part_rows(part, parts):
            _row_copy(ybuf.at[slot], r, y_hbm, dst_row(tile, r), ssem.at[slot]).start(
                priority=r % DMA_QUEUES)

    def wait_gather(tile):
        slot = lax.rem(tile, GATHER_SLOTS)
        pltpu.make_async_copy(h_hbm.at[pl.ds(0, tm * SUBLANES)],
                              xbuf.at[slot, pl.ds(0, tm * SUBLANES)], gsem.at[slot]).wait()

    def wait_scatter(tile):
        slot = lax.rem(tile + LEAD_TILES, SCATTER_SLOTS)
        pltpu.make_async_copy(ybuf.at[slot], y_hbm.at[pl.ds(0, tm * SUBLANES)], ssem.at[slot]).wait()

    @pl.when(i == 0)
    def _():
        ybuf[...] = jnp.zeros(ybuf.shape, F32)
        xbuf[:, tm * SUBLANES:, :] = jnp.zeros((GATHER_SLOTS, SUBLANES, LANES), U32)
        gather(0)
        gather(1)
        scatter(-3)
        scatter(-2)

    def move_rows(part):
        scatter(i - 1, part, DMA_GROUPS)
        gather(i + 2, part, DMA_GROUPS)
        spare = xbuf[lax.rem(i + 2 + GATHER_SLOTS, GATHER_SLOTS), tm * SUBLANES:, :]
        return ((spare >> 16) >> 16)[0:1]

    wait_scatter(i - 3)
    wait_gather(i)
    rows = xbuf.at[lax.rem(i, GATHER_SLOTS)]
    slabs = [rows[pl.ds(j, tm, stride=SUBLANES), :] for j in range(INFO_SLAB + 1)]
    packed = jnp.concatenate(slabs[:PACKED_SLABS], axis=1)
    xg = jnp.concatenate(
        [pltpu.unpack_elementwise(packed, index=k, packed_dtype=BF16, unpacked_dtype=F32)
         for k in range(2)], axis=1).astype(BF16)
    info_a = lax.bitcast_convert_type(slabs[INFO_SLAB] | move_rows(0), F32)

    def act(wg_ref, w):
        gu = jnp.dot(xg, wg_ref[0], preferred_element_type=F32)
        gate = gu[:, :D_EXPERT]
        up = gu[:, D_EXPERT:]
        return (gate * jax.nn.sigmoid(gate) * up * w).astype(BF16)

    act_a = act(wga_ref, info_a[:, 1:2])
    info_b = lax.bitcast_convert_type(slabs[INFO_SLAB] | move_rows(1), F32)
    act_b = act(wgb_ref, info_b[:, 2:3])
    zero_lo = lax.bitcast_convert_type(move_rows(2), F32)
    y = (jnp.dot(act_a, wda_ref[0], preferred_element_type=F32)
         + jnp.dot(act_b, wdb_ref[0], preferred_element_type=F32))
    zero_hi = lax.bitcast_convert_type(move_rows(3), F32)
    out = ybuf.at[lax.rem(i, SCATTER_SLOTS)]
    for j in range(SUBLANES):
        zero = zero_lo if j < SUBLANES // 2 else zero_hi
        out[pl.ds(j, tm, stride=SUBLANES), :] = y[:, j * LANES:(j + 1) * LANES] + zero

    @pl.when(i == n - 1)
    def _():
        scatter(i)
        wait_scatter(i - 2)
        wait_scatter(i - 1)
        wait_scatter(i)
        wait_gather(i + 1)
        wait_gather(i + 2)


def _ffn(tile_ea, tile_eb, dst, h_rows, wgu, wd):
    n_tiles = tile_ea.shape[0]
    tm = FFN_ROWS
    tokens = h_rows.shape[0] // SUBLANES
    return pl.pallas_call(
        functools.partial(_ffn_kernel, tokens=tokens),
        out_shape=jax.ShapeDtypeStruct(
            ((tokens + SCATTER_SLOTS * tm) * SUBLANES, LANES), F32),
        grid_spec=pltpu.PrefetchScalarGridSpec(
            num_scalar_prefetch=3,
            grid=(n_tiles,),
            in_specs=[
                pl.BlockSpec(memory_space=pl.ANY),
                pl.BlockSpec((1, D_MODEL, 2 * D_EXPERT), lambda i, ea, eb, ds: (ea[i], 0, 0)),
                pl.BlockSpec((1, D_MODEL, 2 * D_EXPERT), lambda i, ea, eb, ds: (eb[i], 0, 0)),
                pl.BlockSpec((1, D_EXPERT, D_MODEL), lambda i, ea, eb, ds: (ea[i], 0, 0)),
                pl.BlockSpec((1, D_EXPERT, D_MODEL), lambda i, ea, eb, ds: (eb[i], 0, 0)),
            ],
            out_specs=pl.BlockSpec(memory_space=pl.ANY),
            scratch_shapes=[pltpu.VMEM((GATHER_SLOTS, (tm + 1) * SUBLANES, LANES), U32),
                            pltpu.VMEM((SCATTER_SLOTS, tm * SUBLANES, LANES), F32),
                            pltpu.SemaphoreType.DMA((GATHER_SLOTS,)),
                            pltpu.SemaphoreType.DMA((SCATTER_SLOTS,))],
        ),
        compiler_params=pltpu.CompilerParams(
            dimension_semantics=("arbitrary",), vmem_limit_bytes=VMEM_LIMIT),
        name="ffn",
    )(tile_ea, tile_eb, dst, h_rows, wgu, wgu, wd, wd)


def _fin_kernel(x1_ref, y_ref, mod_ref, fg_ref, o_ref):
    x2 = x1_ref[0] + mod_ref[0][5:6] * _rows_to_tokens(y_ref, x1_ref.shape[1])
    ms = jnp.mean(x2 * x2, axis=-1, keepdims=True)
    o_ref[0] = x2 * lax.rsqrt(ms + EPS) * fg_ref[...]


def _fin(x1, y_rows, mod, final_g):
    batch, seq, _ = x1.shape
    tf = FIN_TOKENS
    n_s = seq // tf
    return pl.pallas_call(
        _fin_kernel,
        out_shape=jax.ShapeDtypeStruct((batch, seq, D_MODEL), F32),
        grid=(batch, n_s),
        in_specs=[
            pl.BlockSpec((1, tf, D_MODEL), lambda b, s: (b, s, 0)),
            pl.BlockSpec((tf * SUBLANES, LANES), lambda b, s: (b * n_s + s, 0)),
            pl.BlockSpec((1, 6, D_MODEL), lambda b, s: (b, 0, 0)),
            pl.BlockSpec((1, D_MODEL), lambda b, s: (0, 0)),
        ],
        out_specs=pl.BlockSpec((1, tf, D_MODEL), lambda b, s: (b, s, 0)),
        compiler_params=pltpu.CompilerParams(
            dimension_semantics=("arbitrary", "arbitrary"), vmem_limit_bytes=VMEM_LIMIT),
        name="fin",
    )(x1, y_rows, mod, final_g)


def _plan(cls, n_tiles):
    tm = FFN_ROWS
    onehot = (cls[:, None] == jnp.arange(N_CLASSES, dtype=jnp.int32)[None, :]).astype(jnp.int32)
    csum = jnp.cumsum(onehot, axis=0)
    rank = jnp.take_along_axis(csum, cls[:, None], axis=1)[:, 0] - 1
    counts = csum[-1]
    tiles = (counts + tm - 1) // tm
    tile_end = jnp.cumsum(tiles)
    tile_start = tile_end - tiles
    pos = tile_start[cls] * tm + rank
    j = jnp.arange(n_tiles, dtype=jnp.int32)
    total = tile_end[-1]
    j_eff = jnp.minimum(j, total - 1)
    tile_cls = jnp.sum((tile_end[None, :] <= j_eff[:, None]).astype(jnp.int32), axis=1)
    group = tile_cls // N_PAIRS
    pair = tile_cls % N_PAIRS
    ea = group * GROUP_SIZE + jnp.asarray(PAIR_A, jnp.int32)[pair]
    eb = group * GROUP_SIZE + jnp.asarray(PAIR_B, jnp.int32)[pair]
    tokens = cls.shape[0]
    rows = jnp.arange(-LEAD_TILES * tm, (n_tiles + TAIL_TILES) * tm, dtype=jnp.int32)
    dump = tokens + ((rows // tm) % SCATTER_SLOTS) * tm + rows % tm
    dst = dump.at[pos.astype(jnp.int32) + LEAD_TILES * tm].set(
        jnp.arange(tokens, dtype=jnp.int32), unique_indices=True)
    return dst, ea, eb


def kernel(x, c, w_in, conv_w, pool_w, pool_scale, w_out, norm1_g, norm2_g, w_ada, b_ada,
           w_router, router_bias, w_gate_up, w_down, final_g):
    batch, seq, _ = x.shape
    depth = w_in.shape[0]
    tokens = batch * seq
    n_tiles = tokens // FFN_ROWS + N_CLASSES

    mod = _ada(c, w_ada, b_ada).reshape(depth, batch, 6, D_MODEL)
    wr_t = w_router.T.astype(BF16)
    rb = router_bias.reshape(N_EXPERTS, 1).astype(F32)
    fg = final_g.reshape(1, D_MODEL)
    z = jnp.zeros((depth, POOL_GROUP, POOL_GROUP), F32)
    pw_blocks = jnp.stack([
        jnp.concatenate([jnp.concatenate([pool_w[:, 2 * k], z], axis=2),
                         jnp.concatenate([z, pool_w[:, 2 * k + 1]], axis=2)], axis=1)
        for k in range(2)], axis=1).astype(BF16)

    prev = None
    for l in range(depth):
        x, h_rows, info = _mix(
            x, prev, mod[l], norm1_g[l].reshape(1, -1), norm2_g[l].reshape(1, -1),
            w_in[l].astype(BF16), conv_w[l], pw_blocks[l], pool_scale[l].reshape(1, -1),
            w_out[l].astype(BF16), wr_t, rb)
        dst, ea, eb = _plan(info[0].astype(jnp.int32), n_tiles)
        y_rows = _ffn(ea, eb, dst, h_rows, w_gate_up[l].astype(BF16), w_down[l].astype(BF16))
        prev = (y_rows, mod[l])
    return _fin(x, y_rows, mod[depth - 1], fg)
```

```python
import functools

import jax
import jax.numpy as jnp
from jax import lax
from jax.experimental import pallas as pl
from jax.experimental.pallas import tpu as pltpu

D_MODEL = 1024
CONV_WIDTH = 512
POOL_WIDTH = 512
IN_COLS = 3 * CONV_WIDTH + POOL_WIDTH
POOL_WINDOWS = (2, 4, 8, 16)
POOL_GROUP = 128
N_EXPERTS = 16
N_GROUPS = 4
GROUP_SIZE = 4
D_EXPERT = 512
EPS = 1e-6

LANES = 128
SUBLANES = 8
PACKED_SLABS = D_MODEL // 2 // LANES
INFO_SLAB = PACKED_SLABS
CONV_HALO = SUBLANES
POOL_HALO = 2 * SUBLANES

PAIR_A = (0, 0, 0, 1, 1, 2)
PAIR_B = (1, 2, 3, 2, 3, 3)
N_PAIRS = len(PAIR_A)
N_CLASSES = N_GROUPS * N_PAIRS

MIX_TOKENS = 512
FFN_ROWS = 256
FIN_TOKENS = 512
FFN_SLOTS = 3
FFN_AHEAD = FFN_SLOTS - 1
TOKEN_SLOTS = 2
DMA_GROUPS = 4
VMEM_LIMIT = 56 * 1024 * 1024

F32 = jnp.float32
BF16 = jnp.bfloat16
U32 = jnp.uint32


def _norm_mod(v, gain, shift, scale):
    ms = jnp.mean(v * v, axis=-1, keepdims=True)
    y = v * lax.rsqrt(ms + EPS) * gain
    return y * (1.0 + scale) + shift


def _rows_to_tokens(rows_ref, n):
    return jnp.concatenate(
        [rows_ref[pl.ds(j, n, stride=SUBLANES), :] for j in range(SUBLANES)], axis=1)


def _row_copy(src, src_row, dst, dst_row, sem):
    return pltpu.make_async_copy(
        src.at[pl.ds(pl.multiple_of(src_row * SUBLANES, SUBLANES), SUBLANES)],
        dst.at[pl.ds(pl.multiple_of(dst_row * SUBLANES, SUBLANES), SUBLANES)], sem)


class _RowGather:
    def __init__(self, idx_ref, src_hbm, buf, sems, n):
        self.idx_ref, self.src, self.buf, self.sems, self.n = idx_ref, src_hbm, buf, sems, n
        self.slots = buf.shape[0]

    def zero_spare(self):
        spare = (self.slots, SUBLANES, LANES)
        self.buf[:, self.n * SUBLANES:, :] = jnp.zeros(spare, self.buf.dtype)

    def start(self, tile, part=0, parts=1):
        slot = lax.rem(tile, self.slots)
        for r in range(part * self.n // parts, (part + 1) * self.n // parts):
            _row_copy(self.src, self.idx_ref[tile * self.n + r], self.buf.at[slot], r,
                      self.sems.at[slot]).start()

    def start_anchored(self, tile, part):
        self.start(tile, part, DMA_GROUPS)
        spare = self.buf[lax.rem(tile + self.slots, self.slots), self.n * SUBLANES:, :]
        bits = spare if spare.dtype == U32 else lax.bitcast_convert_type(spare, U32)
        return ((bits >> 16) >> 16)[0:1, 0:1]

    def wait(self, tile):
        slot = lax.rem(tile, self.slots)
        rows = pl.ds(0, self.n * SUBLANES)
        pltpu.make_async_copy(self.src.at[rows], self.buf.at[slot, rows], self.sems.at[slot]).wait()

    def rows(self, tile):
        return self.buf.at[lax.rem(tile, self.slots)]


def _zero_f32(zero_bits):
    return lax.bitcast_convert_type(zero_bits, F32)


def _linear_step():
    return (pl.program_id(0) * pl.num_programs(1) + pl.program_id(1),
            pl.num_programs(0) * pl.num_programs(1))


def _ada_kernel(c_ref, w_ref, b_ref, o_ref):
    c = c_ref[...]
    c_act = (c * jax.nn.sigmoid(c)).astype(BF16)
    o_ref[0] = jnp.dot(c_act, w_ref[0].astype(BF16), preferred_element_type=F32) + b_ref[0]


def _ada(c, w_ada, b_ada):
    depth = w_ada.shape[0]
    batch = c.shape[0]
    n_col = w_ada.shape[2] // D_MODEL
    return pl.pallas_call(
        _ada_kernel,
        out_shape=jax.ShapeDtypeStruct((depth, batch, n_col * D_MODEL), F32),
        grid=(depth, n_col),
        in_specs=[
            pl.BlockSpec((batch, D_MODEL), lambda l, j: (0, 0)),
            pl.BlockSpec((1, D_MODEL, D_MODEL), lambda l, j: (l, 0, j)),
            pl.BlockSpec((1, 1, D_MODEL), lambda l, j: (l, 0, j)),
        ],
        out_specs=pl.BlockSpec((1, batch, D_MODEL), lambda l, j: (l, 0, j)),
        compiler_params=pltpu.CompilerParams(
            dimension_semantics=("arbitrary", "arbitrary"), vmem_limit_bytes=VMEM_LIMIT),
        name="ada",
    )(c, w_ada, b_ada.reshape(depth, 1, -1))


def _route(logits_t, bias):
    s = jax.nn.sigmoid(logits_t)
    sel = s + bias
    sel_r = [sel[e:e + 1] for e in range(N_EXPERTS)]
    s_r = [s[e:e + 1] for e in range(N_EXPERTS)]

    def group_score(g):
        r = sel_r[GROUP_SIZE * g:GROUP_SIZE * (g + 1)]
        best = r[PAIR_A[0]] + r[PAIR_B[0]]
        for p in range(1, N_PAIRS):
            best = jnp.maximum(best, r[PAIR_A[p]] + r[PAIR_B[p]])
        return best

    best_v = group_score(0)
    best_g = jnp.zeros_like(best_v)
    for g in range(1, N_GROUPS):
        gs = group_score(g)
        upd = gs > best_v
        best_v = jnp.where(upd, gs, best_v)
        best_g = jnp.where(upd, float(g), best_g)

    def pick(rows, i):
        out = rows[i]
        for g in range(1, N_GROUPS):
            out = jnp.where(best_g == float(g), rows[GROUP_SIZE * g + i], out)
        return out

    selg = [pick(sel_r, i) for i in range(GROUP_SIZE)]
    sg = [pick(s_r, i) for i in range(GROUP_SIZE)]
    chosen = []
    for i in range(GROUP_SIZE):
        beaten = jnp.zeros_like(best_v)
        for j in range(GROUP_SIZE):
            if j == i:
                continue
            wins = selg[j] > selg[i]
            if j < i:
                wins = wins | (selg[j] == selg[i])
            beaten = beaten + wins.astype(F32)
        chosen.append(beaten < 2.0)
    m0, m1, m2, m3 = chosen
    pair = jnp.where(m0, jnp.where(m1, 0.0, jnp.where(m2, 1.0, 2.0)),
                     jnp.where(m1, jnp.where(m2, 3.0, 4.0), 5.0))
    s_a = jnp.where(m0, sg[0], jnp.where(m1, sg[1], sg[2]))
    s_b = jnp.where(m3, sg[3], jnp.where(m2, sg[2], sg[1]))
    tot = s_a + s_b
    cls = best_g * float(N_PAIRS) + pair
    zero = jnp.zeros_like(cls)
    return jnp.concatenate([cls, s_a / tot, s_b / tot] + [zero] * (SUBLANES - 3), axis=0)


def _mix_kernel(*refs, prev_moe):
    if prev_moe:
        pos_ref, x_ref, y_hbm, pmod_ref, *refs = refs
        *refs, ybuf, ysem = refs
    else:
        x_ref, *refs = refs
    (mod_ref, n1_ref, n2_ref, win_ref, cw_ref, pw_ref, ps_ref, wout_ref, wr_ref, rb_ref,
     x1_ref, h_ref, ri_ref, gbuf, vbuf) = refs
    ts = x_ref.shape[1]
    st = pl.program_id(1)
    mod = mod_ref[0]
    sh1, sc1, g1, sh2, sc2 = (mod[i:i + 1] for i in range(5))
    cw = cw_ref[...]
    ps = ps_ref[...]
    x = x_ref[0]

    if prev_moe:
        step, n_steps = _linear_step()
        prev_rows = _RowGather(pos_ref, y_hbm, ybuf, ysem, ts)

        @pl.when(step == 0)
        def _():
            prev_rows.zero_spare()
            prev_rows.start(0)

        prev_rows.wait(step)
        x = x + pmod_ref[0][5:6] * _rows_to_tokens(prev_rows.rows(step), ts)
        cw = cw + _zero_f32(prev_rows.start_anchored(step + 1, 0))

    @pl.when(st == 0)
    def _():
        gbuf[0:CONV_HALO] = jnp.zeros((CONV_HALO, CONV_WIDTH), F32)
        vbuf[0:POOL_HALO] = jnp.zeros((POOL_HALO, POOL_WIDTH), F32)

    h = _norm_mod(x, n1_ref[...], sh1, sc1).astype(BF16)
    proj = jnp.dot(h, win_ref[...], preferred_element_type=F32)
    u_b = proj[:, 0:CONV_WIDTH]
    u_c = proj[:, CONV_WIDTH:2 * CONV_WIDTH]
    u_h = proj[:, 2 * CONV_WIDTH:3 * CONV_WIDTH]
    v = proj[:, 3 * CONV_WIDTH:]
    if prev_moe:
        ps = ps + _zero_f32(prev_rows.start_anchored(step + 1, 1))

    g = u_c * u_h
    gbuf[CONV_HALO:CONV_HALO + ts] = g
    vbuf[POOL_HALO:POOL_HALO + ts] = v

    conv = (cw[0:1] * gbuf[CONV_HALO - 2:CONV_HALO - 2 + ts]
            + cw[1:2] * gbuf[CONV_HALO - 1:CONV_HALO - 1 + ts]
            + cw[2:3] * g)
    y_conv = u_b * conv

    t_pos = (st * ts + lax.broadcasted_iota(jnp.int32, (ts, 1), 0) + 1).astype(F32)
    pooled = []
    for gi, win in enumerate(POOL_WINDOWS):
        lo = gi * POOL_GROUP
        v_g = v[:, lo:lo + POOL_GROUP]
        acc = v_g
        for k in range(1, win):
            acc = acc + vbuf[POOL_HALO - k:POOL_HALO - k + ts, lo:lo + POOL_GROUP]
        inv_cnt = 1.0 / jnp.minimum(t_pos, float(win))
        pooled.append(acc * inv_cnt - v_g)
    pooled = jnp.concatenate(pooled, axis=1).astype(BF16)
    half = POOL_WIDTH // 2
    y_pool = jnp.concatenate(
        [jnp.dot(pooled[:, 0:half], pw_ref[0], preferred_element_type=F32),
         jnp.dot(pooled[:, half:], pw_ref[1], preferred_element_type=F32)], axis=1) * ps
    if prev_moe:
        g1 = g1 + _zero_f32(prev_rows.start_anchored(step + 1, 2))

    gbuf[0:CONV_HALO] = gbuf[ts:ts + CONV_HALO]
    vbuf[0:POOL_HALO] = vbuf[ts:ts + POOL_HALO]

    y_mix = jnp.concatenate([y_conv, y_pool], axis=1).astype(BF16)
    x1 = x + g1 * jnp.dot(y_mix, wout_ref[...], preferred_element_type=F32)
    x1_ref[0] = x1
    if prev_moe:
        sh2 = sh2 + _zero_f32(prev_rows.start_anchored(step + 1, 3))

    h2 = _norm_mod(x1, n2_ref[...], sh2, sc2)
    logits_t = lax.dot_general(wr_ref[...], h2.astype(BF16), (((1,), (1,)), ((), ())),
                               preferred_element_type=F32)
    info = _route(logits_t, rb_ref[...])
    ri_ref[...] = info

    packed = pltpu.pack_elementwise([h2[:, :D_MODEL // 2], h2[:, D_MODEL // 2:]], packed_dtype=BF16)
    for j in range(PACKED_SLABS):
        h_ref[pl.ds(j, ts, stride=SUBLANES), :] = packed[:, j * LANES:(j + 1) * LANES]
    info_t = jnp.concatenate([info, jnp.zeros((LANES - SUBLANES, ts), F32)], axis=0).T
    h_ref[pl.ds(INFO_SLAB, ts, stride=SUBLANES), :] = lax.bitcast_convert_type(info_t, U32)
    for j in range(INFO_SLAB + 1, SUBLANES):
        h_ref[pl.ds(j, ts, stride=SUBLANES), :] = jnp.zeros((ts, LANES), U32)

    if prev_moe:
        @pl.when(step == n_steps - 1)
        def _():
            prev_rows.wait(step + 1)


def _mix(x, prev, mod, n1, n2, win, cw, pw, ps, wout, wr, rb):
    batch, seq, _ = x.shape
    ts = MIX_TOKENS
    n_s = seq // ts
    tokens = batch * seq
    const = lambda shape: pl.BlockSpec(shape, lambda b, s, *_: (0,) * len(shape))
    prev_specs = [] if prev is None else [
        pl.BlockSpec(memory_space=pl.ANY),
        pl.BlockSpec((1, 6, D_MODEL), lambda b, s, *_: (b, 0, 0))]
    prev_scratch = [] if prev is None else [
        pltpu.VMEM((TOKEN_SLOTS, (ts + 1) * SUBLANES, LANES), F32),
        pltpu.SemaphoreType.DMA((TOKEN_SLOTS,))]
    return pl.pallas_call(
        functools.partial(_mix_kernel, prev_moe=prev is not None),
        out_shape=(
            jax.ShapeDtypeStruct((batch, seq, D_MODEL), F32),
            jax.ShapeDtypeStruct((tokens * SUBLANES, LANES), U32),
            jax.ShapeDtypeStruct((SUBLANES, tokens), F32),
        ),
        grid_spec=pltpu.PrefetchScalarGridSpec(
            num_scalar_prefetch=0 if prev is None else 1,
            grid=(batch, n_s),
            in_specs=[
                pl.BlockSpec((1, ts, D_MODEL), lambda b, s, *_: (b, s, 0)),
                *prev_specs,
                pl.BlockSpec((1, 6, D_MODEL), lambda b, s, *_: (b, 0, 0)),
                const((1, D_MODEL)), const((1, D_MODEL)),
                const((D_MODEL, IN_COLS)),
                const((3, CONV_WIDTH)),
                const((2, POOL_WIDTH // 2, POOL_WIDTH // 2)),
                const((1, POOL_WIDTH)),
                const((D_MODEL, D_MODEL)),
                const((N_EXPERTS, D_MODEL)),
                const((N_EXPERTS, 1)),
            ],
            out_specs=(
                pl.BlockSpec((1, ts, D_MODEL), lambda b, s, *_: (b, s, 0)),
                pl.BlockSpec((ts * SUBLANES, LANES), lambda b, s, *_: (b * n_s + s, 0)),
                pl.BlockSpec((SUBLANES, ts), lambda b, s, *_: (0, b * n_s + s)),
            ),
            scratch_shapes=[
                pltpu.VMEM((ts + CONV_HALO, CONV_WIDTH), F32),
                pltpu.VMEM((ts + POOL_HALO, POOL_WIDTH), F32),
                *prev_scratch,
            ],
        ),
        compiler_params=pltpu.CompilerParams(
            dimension_semantics=("arbitrary", "arbitrary"), vmem_limit_bytes=VMEM_LIMIT),
        name="mix",
    )(*(() if prev is None else prev[:1]), x, *(() if prev is None else prev[1:]),
      mod, n1, n2, win, cw, pw, ps, wout, wr, rb)


def _ffn_kernel(ea_ref, eb_ref, inv_ref, h_hbm, wga_ref, wgb_ref, wda_ref, wdb_ref, y_ref,
                xbuf, sems):
    del ea_ref, eb_ref
    tm = FFN_ROWS
    i = pl.program_id(0)
    n = pl.num_programs(0)
    tok_rows = _RowGather(inv_ref, h_hbm, xbuf, sems, tm)

    @pl.when(i == 0)
    def _():
        tok_rows.zero_spare()
        for tile in range(FFN_AHEAD):
            tok_rows.start(tile)

    tok_rows.wait(i)
    rows = tok_rows.rows(i)
    slabs = [rows[pl.ds(j, tm, stride=SUBLANES), :] for j in range(INFO_SLAB + 1)]
    packed = jnp.concatenate(slabs[:PACKED_SLABS], axis=1)
    xg = jnp.concatenate(
        [pltpu.unpack_elementwise(packed, index=k, packed_dtype=BF16, unpacked_dtype=F32)
         for k in range(2)], axis=1).astype(BF16)
    info_a = lax.bitcast_convert_type(
        slabs[INFO_SLAB] | tok_rows.start_anchored(i + FFN_AHEAD, 0), F32)

    def act(wg_ref, w):
        gu = jnp.dot(xg, wg_ref[0], preferred_element_type=F32)
        gate = gu[:, :D_EXPERT]
        up = gu[:, D_EXPERT:]
        return (gate * jax.nn.sigmoid(gate) * up * w).astype(BF16)

    act_a = act(wga_ref, info_a[:, 1:2])
    info_b = lax.bitcast_convert_type(
        slabs[INFO_SLAB] | tok_rows.start_anchored(i + FFN_AHEAD, 1), F32)
    act_b = act(wgb_ref, info_b[:, 2:3])
    zero_lo = _zero_f32(tok_rows.start_anchored(i + FFN_AHEAD, 2))
    y = (jnp.dot(act_a, wda_ref[0], preferred_element_type=F32)
         + jnp.dot(act_b, wdb_ref[0], preferred_element_type=F32))
    zero_hi = _zero_f32(tok_rows.start_anchored(i + FFN_AHEAD, 3))
    for j in range(SUBLANES):
        zero = zero_lo if j < SUBLANES // 2 else zero_hi
        y_ref[pl.ds(j, tm, stride=SUBLANES), :] = y[:, j * LANES:(j + 1) * LANES] + zero

    @pl.when(i == n - 1)
    def _():
        for tile in range(1, FFN_AHEAD + 1):
            tok_rows.wait(i + tile)


def _ffn(tile_ea, tile_eb, inv, h_rows, wgu, wd):
    n_tiles = tile_ea.shape[0]
    tm = FFN_ROWS
    return pl.pallas_call(
        _ffn_kernel,
        out_shape=jax.ShapeDtypeStruct((n_tiles * tm * SUBLANES, LANES), F32),
        grid_spec=pltpu.PrefetchScalarGridSpec(
            num_scalar_prefetch=3,
            grid=(n_tiles,),
            in_specs=[
                pl.BlockSpec(memory_space=pl.ANY),
                pl.BlockSpec((1, D_MODEL, 2 * D_EXPERT), lambda i, ea, eb, iv: (ea[i], 0, 0)),
                pl.BlockSpec((1, D_MODEL, 2 * D_EXPERT), lambda i, ea, eb, iv: (eb[i], 0, 0)),
                pl.BlockSpec((1, D_EXPERT, D_MODEL), lambda i, ea, eb, iv: (ea[i], 0, 0)),
                pl.BlockSpec((1, D_EXPERT, D_MODEL), lambda i, ea, eb, iv: (eb[i], 0, 0)),
            ],
            out_specs=pl.BlockSpec((tm * SUBLANES, LANES), lambda i, ea, eb, iv: (i, 0)),
            scratch_shapes=[pltpu.VMEM((FFN_SLOTS, (tm + 1) * SUBLANES, LANES), U32),
                            pltpu.SemaphoreType.DMA((FFN_SLOTS,))],
        ),
        compiler_params=pltpu.CompilerParams(
            dimension_semantics=("arbitrary",), vmem_limit_bytes=VMEM_LIMIT),
        name="ffn",
    )(tile_ea, tile_eb, inv, h_rows, wgu, wgu, wd, wd)


def _fin_kernel(pos_ref, x1_ref, mod_ref, fg_ref, y_hbm, o_ref, ybuf, ysem):
    tf = x1_ref.shape[1]
    step, n_steps = _linear_step()
    last_rows = _RowGather(pos_ref, y_hbm, ybuf, ysem, tf)

    @pl.when(step == 0)
    def _():
        last_rows.zero_spare()
        last_rows.start(0)

    last_rows.start(step + 1)
    last_rows.wait(step)
    x2 = x1_ref[0] + mod_ref[0][5:6] * _rows_to_tokens(last_rows.rows(step), tf)
    ms = jnp.mean(x2 * x2, axis=-1, keepdims=True)
    o_ref[0] = x2 * lax.rsqrt(ms + EPS) * fg_ref[...]

    @pl.when(step == n_steps - 1)
    def _():
        last_rows.wait(step + 1)


def _fin(pos, x1, y_rows, mod, final_g):
    batch, seq, _ = x1.shape
    tf = FIN_TOKENS
    return pl.pallas_call(
        _fin_kernel,
        out_shape=jax.ShapeDtypeStruct((batch, seq, D_MODEL), F32),
        grid_spec=pltpu.PrefetchScalarGridSpec(
            num_scalar_prefetch=1,
            grid=(batch, seq // tf),
            in_specs=[
                pl.BlockSpec((1, tf, D_MODEL), lambda b, s, p: (b, s, 0)),
                pl.BlockSpec((1, 6, D_MODEL), lambda b, s, p: (b, 0, 0)),
                pl.BlockSpec((1, D_MODEL), lambda b, s, p: (0, 0)),
                pl.BlockSpec(memory_space=pl.ANY),
            ],
            out_specs=pl.BlockSpec((1, tf, D_MODEL), lambda b, s, p: (b, s, 0)),
            scratch_shapes=[pltpu.VMEM((TOKEN_SLOTS, (tf + 1) * SUBLANES, LANES), F32),
                            pltpu.SemaphoreType.DMA((TOKEN_SLOTS,))],
        ),
        compiler_params=pltpu.CompilerParams(
            dimension_semantics=("arbitrary", "arbitrary"), vmem_limit_bytes=VMEM_LIMIT),
        name="fin",
    )(pos, x1, mod, final_g, y_rows)


def _plan(cls, n_tiles):
    tm = FFN_ROWS
    tokens = cls.shape[0]
    onehot = (cls[:, None] == jnp.arange(N_CLASSES, dtype=jnp.int32)[None, :]).astype(jnp.int32)
    csum = jnp.cumsum(onehot, axis=0)
    rank = jnp.take_along_axis(csum, cls[:, None], axis=1)[:, 0] - 1
    counts = csum[-1]
    tiles = (counts + tm - 1) // tm
    tile_end = jnp.cumsum(tiles)
    tile_start = tile_end - tiles
    pos = (tile_start[cls] * tm + rank).astype(jnp.int32)
    j = jnp.arange(n_tiles, dtype=jnp.int32)
    total = tile_end[-1]
    j_eff = jnp.minimum(j, total - 1)
    tile_cls = jnp.sum((tile_end[None, :] <= j_eff[:, None]).astype(jnp.int32), axis=1)
    group = tile_cls // N_PAIRS
    pair = tile_cls % N_PAIRS
    ea = group * GROUP_SIZE + jnp.asarray(PAIR_A, jnp.int32)[pair]
    eb = group * GROUP_SIZE + jnp.asarray(PAIR_B, jnp.int32)[pair]
    inv = jnp.zeros(((n_tiles + FFN_AHEAD) * tm,), jnp.int32).at[pos].set(
        jnp.arange(tokens, dtype=jnp.int32), unique_indices=True)
    pos_ahead = jnp.concatenate([pos, jnp.zeros((max(MIX_TOKENS, FIN_TOKENS),), jnp.int32)])
    return pos_ahead, inv, ea, eb


def kernel(x, c, w_in, conv_w, pool_w, pool_scale, w_out, norm1_g, norm2_g, w_ada, b_ada,
           w_router, router_bias, w_gate_up, w_down, final_g):
    batch, seq, _ = x.shape
    depth = w_in.shape[0]
    tokens = batch * seq
    n_tiles = tokens // FFN_ROWS + N_CLASSES

    mod = _ada(c, w_ada, b_ada).reshape(depth, batch, 6, D_MODEL)
    wr_t = w_router.T.astype(BF16)
    rb = router_bias.reshape(N_EXPERTS, 1).astype(F32)
    fg = final_g.reshape(1, D_MODEL)
    z = jnp.zeros((depth, POOL_GROUP, POOL_GROUP), F32)
    pw_blocks = jnp.stack([
        jnp.concatenate([jnp.concatenate([pool_w[:, 2 * k], z], axis=2),
                         jnp.concatenate([z, pool_w[:, 2 * k + 1]], axis=2)], axis=1)
        for k in range(2)], axis=1).astype(BF16)

    prev = None
    for l in range(depth):
        x, h_rows, info = _mix(
            x, prev, mod[l], norm1_g[l].reshape(1, -1), norm2_g[l].reshape(1, -1),
            w_in[l].astype(BF16), conv_w[l], pw_blocks[l], pool_scale[l].reshape(1, -1),
            w_out[l].astype(BF16), wr_t, rb)
        pos, inv, ea, eb = _plan(info[0].astype(jnp.int32), n_tiles)
        y_rows = _ffn(ea, eb, inv, h_rows, w_gate_up[l].astype(BF16), w_down[l].astype(BF16))
        prev = (pos, y_rows, mod[l])
    return _fin(pos, x, y_rows, mod[depth - 1], fg)
```

```python
import functools

import jax
import jax.numpy as jnp
from jax import lax
from jax.experimental import pallas as pl
from jax.experimental.pallas import tpu as pltpu

D_MODEL = 1024
CONV_WIDTH = 512
POOL_WIDTH = 512
IN_COLS = 3 * CONV_WIDTH + POOL_WIDTH
POOL_WINDOWS = (2, 4, 8, 16)
POOL_GROUP = 128
N_EXPERTS = 16
N_GROUPS = 4
GROUP_SIZE = 4
D_EXPERT = 512
EPS = 1e-6

LANES = 128
SUBLANES = 8
PACKED_SLABS = D_MODEL // 2 // LANES
INFO_SLAB = PACKED_SLABS
CONV_HALO = SUBLANES
POOL_HALO = 2 * SUBLANES

PAIR_A = (0, 0, 0, 1, 1, 2)
PAIR_B = (1, 2, 3, 2, 3, 3)
N_PAIRS = len(PAIR_A)
N_CLASSES = N_GROUPS * N_PAIRS

MIX_TOKENS = 512
FFN_ROWS = 256
FIN_TOKENS = 512
FFN_SLOTS = 3
TOKEN_SLOTS = 2
DMA_GROUPS = 4
VMEM_LIMIT = 56 * 1024 * 1024

F32 = jnp.float32
BF16 = jnp.bfloat16
U32 = jnp.uint32


def _norm_mod(v, gain, shift, scale):
    ms = jnp.mean(v * v, axis=-1, keepdims=True)
    y = v * lax.rsqrt(ms + EPS) * gain
    return y * (1.0 + scale) + shift


def _rows_to_tokens(rows_ref, n):
    return jnp.concatenate(
        [rows_ref[pl.ds(j, n, stride=SUBLANES), :] for j in range(SUBLANES)], axis=1)


def _row_copy(src, src_row, dst, dst_row, sem):
    return pltpu.make_async_copy(
        src.at[pl.ds(pl.multiple_of(src_row * SUBLANES, SUBLANES), SUBLANES)],
        dst.at[pl.ds(pl.multiple_of(dst_row * SUBLANES, SUBLANES), SUBLANES)], sem)


class _RowGather:
    IDX_SLOTS = 2

    def __init__(self, idx_hbm, src_hbm, buf, sems, idx_smem, idx_sems, n):
        self.idx_hbm, self.src, self.buf, self.sems = idx_hbm, src_hbm, buf, sems
        self.idx_smem, self.idx_sems, self.n = idx_smem, idx_sems, n
        self.slots = buf.shape[0]
        self.ahead = self.slots - 1

    def _idx_copy(self, tile):
        slot = lax.rem(tile, self.IDX_SLOTS)
        return pltpu.make_async_copy(
            self.idx_hbm.at[tile], self.idx_smem.at[slot], self.idx_sems.at[slot])

    def _start(self, tile, part=0, parts=1):
        slot = lax.rem(tile, self.slots)
        idx_slot = lax.rem(tile, self.IDX_SLOTS)
        for r in range(part * self.n // parts, (part + 1) * self.n // parts):
            _row_copy(self.src, self.idx_smem[idx_slot, 0, r], self.buf.at[slot], r,
                      self.sems.at[slot]).start()

    def _wait(self, tile):
        slot = lax.rem(tile, self.slots)
        rows = pl.ds(0, self.n * SUBLANES)
        pltpu.make_async_copy(self.src.at[rows], self.buf.at[slot, rows], self.sems.at[slot]).wait()

    def begin(self, step):
        @pl.when(step == 0)
        def _():
            spare = (self.slots, SUBLANES, LANES)
            self.buf[:, self.n * SUBLANES:, :] = jnp.zeros(spare, self.buf.dtype)
            for tile in range(self.ahead):
                self._idx_copy(tile).start()
                self._idx_copy(tile).wait()
                self._start(tile)
            self._idx_copy(self.ahead).start()

        self._idx_copy(step + self.ahead + 1).start()
        self._idx_copy(step + self.ahead).wait()
        self._wait(step)

    def fetch_group(self, step, part):
        tile = step + self.ahead
        self._start(tile, part, DMA_GROUPS)
        spare = self.buf[lax.rem(tile + self.slots, self.slots), self.n * SUBLANES:, :]
        bits = spare if spare.dtype == U32 else lax.bitcast_convert_type(spare, U32)
        return ((bits >> 16) >> 16)[0:1, 0:1]

    def fetch_all(self, step):
        self._start(step + self.ahead)

    def rows(self, step):
        return self.buf.at[lax.rem(step, self.slots)]

    def end(self, step, n_steps):
        @pl.when(step == n_steps - 1)
        def _():
            for k in range(1, self.ahead + 1):
                self._wait(step + k)
            self._idx_copy(step + self.ahead + 1).wait()

    @staticmethod
    def scratch(slots, n, dtype):
        return [pltpu.VMEM((slots, (n + 1) * SUBLANES, LANES), dtype),
                pltpu.SemaphoreType.DMA((slots,)),
                pltpu.SMEM((_RowGather.IDX_SLOTS, 1, n), jnp.int32),
                pltpu.SemaphoreType.DMA((_RowGather.IDX_SLOTS,))]


def _zero_f32(zero_bits):
    return lax.bitcast_convert_type(zero_bits, F32)


def _linear_step():
    return (pl.program_id(0) * pl.num_programs(1) + pl.program_id(1),
            pl.num_programs(0) * pl.num_programs(1))


def _ada_kernel(c_ref, w_ref, b_ref, o_ref):
    c = c_ref[...]
    c_act = (c * jax.nn.sigmoid(c)).astype(BF16)
    o_ref[0] = jnp.dot(c_act, w_ref[0].astype(BF16), preferred_element_type=F32) + b_ref[0]


def _ada(c, w_ada, b_ada):
    depth = w_ada.shape[0]
    batch = c.shape[0]
    n_col = w_ada.shape[2] // D_MODEL
    return pl.pallas_call(
        _ada_kernel,
        out_shape=jax.ShapeDtypeStruct((depth, batch, n_col * D_MODEL), F32),
        grid=(depth, n_col),
        in_specs=[
            pl.BlockSpec((batch, D_MODEL), lambda l, j: (0, 0)),
            pl.BlockSpec((1, D_MODEL, D_MODEL), lambda l, j: (l, 0, j)),
            pl.BlockSpec((1, 1, D_MODEL), lambda l, j: (l, 0, j)),
        ],
        out_specs=pl.BlockSpec((1, batch, D_MODEL), lambda l, j: (l, 0, j)),
        compiler_params=pltpu.CompilerParams(
            dimension_semantics=("arbitrary", "arbitrary"), vmem_limit_bytes=VMEM_LIMIT),
        name="ada",
    )(c, w_ada, b_ada.reshape(depth, 1, -1))


def _route(logits_t, bias):
    s = jax.nn.sigmoid(logits_t)
    sel = s + bias
    sel_r = [sel[e:e + 1] for e in range(N_EXPERTS)]
    s_r = [s[e:e + 1] for e in range(N_EXPERTS)]

    def group_score(g):
        r = sel_r[GROUP_SIZE * g:GROUP_SIZE * (g + 1)]
        best = r[PAIR_A[0]] + r[PAIR_B[0]]
        for p in range(1, N_PAIRS):
            best = jnp.maximum(best, r[PAIR_A[p]] + r[PAIR_B[p]])
        return best

    best_v = group_score(0)
    best_g = jnp.zeros_like(best_v)
    for g in range(1, N_GROUPS):
        gs = group_score(g)
        upd = gs > best_v
        best_v = jnp.where(upd, gs, best_v)
        best_g = jnp.where(upd, float(g), best_g)

    def pick(rows, i):
        out = rows[i]
        for g in range(1, N_GROUPS):
            out = jnp.where(best_g == float(g), rows[GROUP_SIZE * g + i], out)
        return out

    selg = [pick(sel_r, i) for i in range(GROUP_SIZE)]
    sg = [pick(s_r, i) for i in range(GROUP_SIZE)]
    chosen = []
    for i in range(GROUP_SIZE):
        beaten = jnp.zeros_like(best_v)
        for j in range(GROUP_SIZE):
            if j == i:
                continue
            wins = selg[j] > selg[i]
            if j < i:
                wins = wins | (selg[j] == selg[i])
            beaten = beaten + wins.astype(F32)
        chosen.append(beaten < 2.0)
    m0, m1, m2, m3 = chosen
    pair = jnp.where(m0, jnp.where(m1, 0.0, jnp.where(m2, 1.0, 2.0)),
                     jnp.where(m1, jnp.where(m2, 3.0, 4.0), 5.0))
    s_a = jnp.where(m0, sg[0], jnp.where(m1, sg[1], sg[2]))
    s_b = jnp.where(m3, sg[3], jnp.where(m2, sg[2], sg[1]))
    tot = s_a + s_b
    cls = best_g * float(N_PAIRS) + pair
    zero = jnp.zeros_like(cls)
    return jnp.concatenate([cls, s_a / tot, s_b / tot] + [zero] * (SUBLANES - 3), axis=0)


def _mix_kernel(*refs, prev_moe):
    if prev_moe:
        x_ref, pos_hbm, y_hbm, pmod_ref, *refs = refs
        *refs, ybuf, ysem, idx_smem, idx_sems = refs
    else:
        x_ref, *refs = refs
    (mod_ref, n1_ref, n2_ref, win_ref, cw_ref, pw_ref, ps_ref, wout_ref, wr_ref, rb_ref,
     x1_ref, h_ref, ri_ref, gbuf, vbuf) = refs
    ts = x_ref.shape[1]
    st = pl.program_id(1)
    mod = mod_ref[0]
    sh1, sc1, g1, sh2, sc2 = (mod[i:i + 1] for i in range(5))
    cw = cw_ref[...]
    ps = ps_ref[...]
    x = x_ref[0]

    if prev_moe:
        step, n_steps = _linear_step()
        prev_rows = _RowGather(pos_hbm, y_hbm, ybuf, ysem, idx_smem, idx_sems, ts)
        prev_rows.begin(step)
        x = x + pmod_ref[0][5:6] * _rows_to_tokens(prev_rows.rows(step), ts)
        cw = cw + _zero_f32(prev_rows.fetch_group(step, 0))

    @pl.when(st == 0)
    def _():
        gbuf[0:CONV_HALO] = jnp.zeros((CONV_HALO, CONV_WIDTH), F32)
        vbuf[0:POOL_HALO] = jnp.zeros((POOL_HALO, POOL_WIDTH), F32)

    h = _norm_mod(x, n1_ref[...], sh1, sc1).astype(BF16)
    proj = jnp.dot(h, win_ref[...], preferred_element_type=F32)
    u_b = proj[:, 0:CONV_WIDTH]
    u_c = proj[:, CONV_WIDTH:2 * CONV_WIDTH]
    u_h = proj[:, 2 * CONV_WIDTH:3 * CONV_WIDTH]
    v = proj[:, 3 * CONV_WIDTH:]
    if prev_moe:
        ps = ps + _zero_f32(prev_rows.fetch_group(step, 1))

    g = u_c * u_h
    gbuf[CONV_HALO:CONV_HALO + ts] = g
    vbuf[POOL_HALO:POOL_HALO + ts] = v

    conv = (cw[0:1] * gbuf[CONV_HALO - 2:CONV_HALO - 2 + ts]
            + cw[1:2] * gbuf[CONV_HALO - 1:CONV_HALO - 1 + ts]
            + cw[2:3] * g)
    y_conv = u_b * conv

    t_pos = (st * ts + lax.broadcasted_iota(jnp.int32, (ts, 1), 0) + 1).astype(F32)
    pooled = []
    for gi, win in enumerate(POOL_WINDOWS):
        lo = gi * POOL_GROUP
        v_g = v[:, lo:lo + POOL_GROUP]
        acc = v_g
        for k in range(1, win):
            acc = acc + vbuf[POOL_HALO - k:POOL_HALO - k + ts, lo:lo + POOL_GROUP]
        inv_cnt = 1.0 / jnp.minimum(t_pos, float(win))
        pooled.append(acc * inv_cnt - v_g)
    pooled = jnp.concatenate(pooled, axis=1).astype(BF16)
    half = POOL_WIDTH // 2
    y_pool = jnp.concatenate(
        [jnp.dot(pooled[:, 0:half], pw_ref[0], preferred_element_type=F32),
         jnp.dot(pooled[:, half:], pw_ref[1], preferred_element_type=F32)], axis=1) * ps
    if prev_moe:
        g1 = g1 + _zero_f32(prev_rows.fetch_group(step, 2))

    gbuf[0:CONV_HALO] = gbuf[ts:ts + CONV_HALO]
    vbuf[0:POOL_HALO] = vbuf[ts:ts + POOL_HALO]

    y_mix = jnp.concatenate([y_conv, y_pool], axis=1).astype(BF16)
    x1 = x + g1 * jnp.dot(y_mix, wout_ref[...], preferred_element_type=F32)
    x1_ref[0] = x1
    if prev_moe:
        sh2 = sh2 + _zero_f32(prev_rows.fetch_group(step, 3))

    h2 = _norm_mod(x1, n2_ref[...], sh2, sc2)
    logits_t = lax.dot_general(wr_ref[...], h2.astype(BF16), (((1,), (1,)), ((), ())),
                               preferred_element_type=F32)
    info = _route(logits_t, rb_ref[...])
    ri_ref[...] = info

    packed = pltpu.pack_elementwise([h2[:, :D_MODEL // 2], h2[:, D_MODEL // 2:]], packed_dtype=BF16)
    for j in range(PACKED_SLABS):
        h_ref[pl.ds(j, ts, stride=SUBLANES), :] = packed[:, j * LANES:(j + 1) * LANES]
    info_t = jnp.concatenate([info, jnp.zeros((LANES - SUBLANES, ts), F32)], axis=0).T
    h_ref[pl.ds(INFO_SLAB, ts, stride=SUBLANES), :] = lax.bitcast_convert_type(info_t, U32)
    for j in range(INFO_SLAB + 1, SUBLANES):
        h_ref[pl.ds(j, ts, stride=SUBLANES), :] = jnp.zeros((ts, LANES), U32)

    if prev_moe:
        prev_rows.end(step, n_steps)


def _mix(x, prev, mod, n1, n2, win, cw, pw, ps, wout, wr, rb):
    batch, seq, _ = x.shape
    ts = MIX_TOKENS
    n_s = seq // ts
    tokens = batch * seq
    const = lambda shape: pl.BlockSpec(shape, lambda b, s: (0,) * len(shape))
    prev_specs = [] if prev is None else [
        pl.BlockSpec(memory_space=pl.ANY),
        pl.BlockSpec(memory_space=pl.ANY),
        pl.BlockSpec((1, 6, D_MODEL), lambda b, s: (b, 0, 0))]
    prev_scratch = [] if prev is None else _RowGather.scratch(TOKEN_SLOTS, ts, F32)
    return pl.pallas_call(
        functools.partial(_mix_kernel, prev_moe=prev is not None),
        out_shape=(
            jax.ShapeDtypeStruct((batch, seq, D_MODEL), F32),
            jax.ShapeDtypeStruct((tokens * SUBLANES, LANES), U32),
            jax.ShapeDtypeStruct((SUBLANES, tokens), F32),
        ),
        grid_spec=pltpu.PrefetchScalarGridSpec(
            num_scalar_prefetch=0,
            grid=(batch, n_s),
            in_specs=[
                pl.BlockSpec((1, ts, D_MODEL), lambda b, s: (b, s, 0)),
                *prev_specs,
                pl.BlockSpec((1, 6, D_MODEL), lambda b, s: (b, 0, 0)),
                const((1, D_MODEL)), const((1, D_MODEL)),
                const((D_MODEL, IN_COLS)),
                const((3, CONV_WIDTH)),
                const((2, POOL_WIDTH // 2, POOL_WIDTH // 2)),
                const((1, POOL_WIDTH)),
                const((D_MODEL, D_MODEL)),
                const((N_EXPERTS, D_MODEL)),
                const((N_EXPERTS, 1)),
            ],
            out_specs=(
                pl.BlockSpec((1, ts, D_MODEL), lambda b, s: (b, s, 0)),
                pl.BlockSpec((ts * SUBLANES, LANES), lambda b, s: (b * n_s + s, 0)),
                pl.BlockSpec((SUBLANES, ts), lambda b, s: (0, b * n_s + s)),
            ),
            scratch_shapes=[
                pltpu.VMEM((ts + CONV_HALO, CONV_WIDTH), F32),
                pltpu.VMEM((ts + POOL_HALO, POOL_WIDTH), F32),
                *prev_scratch,
            ],
        ),
        compiler_params=pltpu.CompilerParams(
            dimension_semantics=("arbitrary", "arbitrary"), vmem_limit_bytes=VMEM_LIMIT),
        name="mix",
    )(x, *(() if prev is None else prev), mod, n1, n2, win, cw, pw, ps, wout, wr, rb)


def _ffn_kernel(ea_ref, eb_ref, inv_hbm, h_hbm, wga_ref, wgb_ref, wda_ref, wdb_ref, y_ref,
                xbuf, sems, idx_smem, idx_sems):
    del ea_ref, eb_ref
    tm = FFN_ROWS
    i = pl.program_id(0)
    tok_rows = _RowGather(inv_hbm, h_hbm, xbuf, sems, idx_smem, idx_sems, tm)
    tok_rows.begin(i)
    rows = tok_rows.rows(i)
    slabs = [rows[pl.ds(j, tm, stride=SUBLANES), :] for j in range(INFO_SLAB + 1)]
    packed = jnp.concatenate(slabs[:PACKED_SLABS], axis=1)
    xg = jnp.concatenate(
        [pltpu.unpack_elementwise(packed, index=k, packed_dtype=BF16, unpacked_dtype=F32)
         for k in range(2)], axis=1).astype(BF16)
    info_a = lax.bitcast_convert_type(slabs[INFO_SLAB] | tok_rows.fetch_group(i, 0), F32)

    def act(wg_ref, w):
        gu = jnp.dot(xg, wg_ref[0], preferred_element_type=F32)
        gate = gu[:, :D_EXPERT]
        up = gu[:, D_EXPERT:]
        return (gate * jax.nn.sigmoid(gate) * up * w).astype(BF16)

    act_a = act(wga_ref, info_a[:, 1:2])
    info_b = lax.bitcast_convert_type(slabs[INFO_SLAB] | tok_rows.fetch_group(i, 1), F32)
    act_b = act(wgb_ref, info_b[:, 2:3])
    zero_lo = _zero_f32(tok_rows.fetch_group(i, 2))
    y = (jnp.dot(act_a, wda_ref[0], preferred_element_type=F32)
         + jnp.dot(act_b, wdb_ref[0], preferred_element_type=F32))
    zero_hi = _zero_f32(tok_rows.fetch_group(i, 3))
    for j in range(SUBLANES):
        zero = zero_lo if j < SUBLANES // 2 else zero_hi
        y_ref[pl.ds(j, tm, stride=SUBLANES), :] = y[:, j * LANES:(j + 1) * LANES] + zero
    tok_rows.end(i, pl.num_programs(0))


def _ffn(tile_ea, tile_eb, inv, h_rows, wgu, wd):
    n_tiles = tile_ea.shape[0]
    tm = FFN_ROWS
    return pl.pallas_call(
        _ffn_kernel,
        out_shape=jax.ShapeDtypeStruct((n_tiles * tm * SUBLANES, LANES), F32),
        grid_spec=pltpu.PrefetchScalarGridSpec(
            num_scalar_prefetch=2,
            grid=(n_tiles,),
            in_specs=[
                pl.BlockSpec(memory_space=pl.ANY),
                pl.BlockSpec(memory_space=pl.ANY),
                pl.BlockSpec((1, D_MODEL, 2 * D_EXPERT), lambda i, ea, eb: (ea[i], 0, 0)),
                pl.BlockSpec((1, D_MODEL, 2 * D_EXPERT), lambda i, ea, eb: (eb[i], 0, 0)),
                pl.BlockSpec((1, D_EXPERT, D_MODEL), lambda i, ea, eb: (ea[i], 0, 0)),
                pl.BlockSpec((1, D_EXPERT, D_MODEL), lambda i, ea, eb: (eb[i], 0, 0)),
            ],
            out_specs=pl.BlockSpec((tm * SUBLANES, LANES), lambda i, ea, eb: (i, 0)),
            scratch_shapes=_RowGather.scratch(FFN_SLOTS, tm, U32),
        ),
        compiler_params=pltpu.CompilerParams(
            dimension_semantics=("arbitrary",), vmem_limit_bytes=VMEM_LIMIT),
        name="ffn",
    )(tile_ea, tile_eb, inv, h_rows, wgu, wgu, wd, wd)


def _fin_kernel(x1_ref, mod_ref, fg_ref, pos_hbm, y_hbm, o_ref, ybuf, ysem, idx_smem, idx_sems):
    tf = x1_ref.shape[1]
    step, n_steps = _linear_step()
    last_rows = _RowGather(pos_hbm, y_hbm, ybuf, ysem, idx_smem, idx_sems, tf)
    last_rows.begin(step)
    last_rows.fetch_all(step)
    x2 = x1_ref[0] + mod_ref[0][5:6] * _rows_to_tokens(last_rows.rows(step), tf)
    ms = jnp.mean(x2 * x2, axis=-1, keepdims=True)
    o_ref[0] = x2 * lax.rsqrt(ms + EPS) * fg_ref[...]
    last_rows.end(step, n_steps)


def _fin(pos, x1, y_rows, mod, final_g):
    batch, seq, _ = x1.shape
    tf = FIN_TOKENS
    return pl.pallas_call(
        _fin_kernel,
        out_shape=jax.ShapeDtypeStruct((batch, seq, D_MODEL), F32),
        grid=(batch, seq // tf),
        in_specs=[
            pl.BlockSpec((1, tf, D_MODEL), lambda b, s: (b, s, 0)),
            pl.BlockSpec((1, 6, D_MODEL), lambda b, s: (b, 0, 0)),
            pl.BlockSpec((1, D_MODEL), lambda b, s: (0, 0)),
            pl.BlockSpec(memory_space=pl.ANY),
            pl.BlockSpec(memory_space=pl.ANY),
        ],
        out_specs=pl.BlockSpec((1, tf, D_MODEL), lambda b, s: (b, s, 0)),
        scratch_shapes=_RowGather.scratch(TOKEN_SLOTS, tf, F32),
        compiler_params=pltpu.CompilerParams(
            dimension_semantics=("arbitrary", "arbitrary"), vmem_limit_bytes=VMEM_LIMIT),
        name="fin",
    )(x1, mod, final_g, pos, y_rows)


def _plan(cls, n_tiles):
    tm = FFN_ROWS
    tokens = cls.shape[0]
    onehot = (cls[:, None] == jnp.arange(N_CLASSES, dtype=jnp.int32)[None, :]).astype(jnp.int32)
    csum = jnp.cumsum(onehot, axis=0)
    rank = jnp.take_along_axis(csum, cls[:, None], axis=1)[:, 0] - 1
    counts = csum[-1]
    tiles = (counts + tm - 1) // tm
    tile_end = jnp.cumsum(tiles)
    tile_start = tile_end - tiles
    pos = (tile_start[cls] * tm + rank).astype(jnp.int32)
    j = jnp.arange(n_tiles, dtype=jnp.int32)
    total = tile_end[-1]
    j_eff = jnp.minimum(j, total - 1)
    tile_cls = jnp.sum((tile_end[None, :] <= j_eff[:, None]).astype(jnp.int32), axis=1)
    group = tile_cls // N_PAIRS
    pair = tile_cls % N_PAIRS
    ea = group * GROUP_SIZE + jnp.asarray(PAIR_A, jnp.int32)[pair]
    eb = group * GROUP_SIZE + jnp.asarray(PAIR_B, jnp.int32)[pair]
    inv = jnp.zeros(((n_tiles + FFN_SLOTS) * tm,), jnp.int32).at[pos].set(
        jnp.arange(tokens, dtype=jnp.int32), unique_indices=True)
    assert MIX_TOKENS == FIN_TOKENS
    pos_ahead = jnp.concatenate([pos, jnp.zeros((TOKEN_SLOTS * MIX_TOKENS,), jnp.int32)])
    return (pos_ahead.reshape(-1, 1, MIX_TOKENS), inv.reshape(-1, 1, tm), ea, eb)


def kernel(x, c, w_in, conv_w, pool_w, pool_scale, w_out, norm1_g, norm2_g, w_ada, b_ada,
           w_router, router_bias, w_gate_up, w_down, final_g):
    batch, seq, _ = x.shape
    depth = w_in.shape[0]
    tokens = batch * seq
    n_tiles = tokens // FFN_ROWS + N_CLASSES

    mod = _ada(c, w_ada, b_ada).reshape(depth, batch, 6, D_MODEL)
    wr_t = w_router.T.astype(BF16)
    rb = router_bias.reshape(N_EXPERTS, 1).astype(F32)
    fg = final_g.reshape(1, D_MODEL)
    z = jnp.zeros((depth, POOL_GROUP, POOL_GROUP), F32)
    pw_blocks = jnp.stack([
        jnp.concatenate([jnp.concatenate([pool_w[:, 2 * k], z], axis=2),
                         jnp.concatenate([z, pool_w[:, 2 * k + 1]], axis=2)], axis=1)
        for k in range(2)], axis=1).astype(BF16)

    prev = None
    for l in range(depth):
        x, h_rows, info = _mix(
            x, prev, mod[l], norm1_g[l].reshape(1, -1), norm2_g[l].reshape(1, -1),
            w_in[l].astype(BF16), conv_w[l], pw_blocks[l], pool_scale[l].reshape(1, -1),
            w_out[l].astype(BF16), wr_t, rb)
        pos, inv, ea, eb = _plan(info[0].astype(jnp.int32), n_tiles)
        y_rows = _ffn(ea, eb, inv, h_rows, w_gate_up[l].astype(BF16), w_down[l].astype(BF16))
        prev = (pos, y_rows, mod[l])
    return _fin(pos, x, y_rows, mod[depth - 1], fg)
```

```python
import functools

import jax
import jax.numpy as jnp
from jax import lax
from jax.experimental import pallas as pl
from jax.experimental.pallas import tpu as pltpu

D_MODEL = 1024
CONV_WIDTH = 512
POOL_WIDTH = 512
IN_COLS = 3 * CONV_WIDTH + POOL_WIDTH
POOL_WINDOWS = (2, 4, 8, 16)
POOL_GROUP = 128
N_EXPERTS = 16
N_GROUPS = 4
GROUP_SIZE = 4
D_EXPERT = 512
EPS = 1e-6

LANES = 128
SUBLANES = 8
PACKED_SLABS = D_MODEL // 2 // LANES
INFO_SLAB = PACKED_SLABS
CONV_HALO = SUBLANES
POOL_HALO = 2 * SUBLANES

PAIR_A = (0, 0, 0, 1, 1, 2)
PAIR_B = (1, 2, 3, 2, 3, 3)
N_PAIRS = len(PAIR_A)
N_CLASSES = N_GROUPS * N_PAIRS

MIX_TOKENS = 512
FFN_ROWS = 256
FIN_TOKENS = 512
FFN_SLOTS = 3
TOKEN_SLOTS = 2
DMA_GROUPS = 4
PACE_LOADS = 2
VMEM_LIMIT = 56 * 1024 * 1024

F32 = jnp.float32
BF16 = jnp.bfloat16
U32 = jnp.uint32


def _norm_mod(v, gain, shift, scale):
    ms = jnp.mean(v * v, axis=-1, keepdims=True)
    y = v * lax.rsqrt(ms + EPS) * gain
    return y * (1.0 + scale) + shift


def _rows_to_tokens(rows_ref, n):
    return jnp.concatenate(
        [rows_ref[pl.ds(j, n, stride=SUBLANES), :] for j in range(SUBLANES)], axis=1)


def _row_copy(src, src_row, dst, dst_row, sem):
    return pltpu.make_async_copy(
        src.at[pl.ds(pl.multiple_of(src_row * SUBLANES, SUBLANES), SUBLANES)],
        dst.at[pl.ds(pl.multiple_of(dst_row * SUBLANES, SUBLANES), SUBLANES)], sem)


class _RowGather:
    IDX_SLOTS = 2

    def __init__(self, idx_hbm, src_hbm, buf, sems, idx_smem, idx_sems, pace_smem, n):
        self.idx_hbm, self.src, self.buf, self.sems = idx_hbm, src_hbm, buf, sems
        self.idx_smem, self.idx_sems, self.pace_smem, self.n = idx_smem, idx_sems, pace_smem, n
        self.slots = buf.shape[0]
        self.ahead = self.slots - 1
        self.pace = jnp.int32(0)

    def _paced_zero(self):
        for _ in range(PACE_LOADS):
            self.pace = self.pace_smem[self.pace]
        return self.pace

    def _idx_copy(self, tile):
        slot = lax.rem(tile, self.IDX_SLOTS)
        return pltpu.make_async_copy(
            self.idx_hbm.at[tile], self.idx_smem.at[slot], self.idx_sems.at[slot])

    def _start(self, tile, part=0, parts=1, paced=False):
        slot = lax.rem(tile, self.slots)
        idx_slot = lax.rem(tile, self.IDX_SLOTS)
        for r in range(part * self.n // parts, (part + 1) * self.n // parts):
            row = self.idx_smem[idx_slot, 0, r]
            if paced:
                row = row + self._paced_zero()
            _row_copy(self.src, row, self.buf.at[slot], r, self.sems.at[slot]).start()

    def _wait(self, tile):
        slot = lax.rem(tile, self.slots)
        rows = pl.ds(0, self.n * SUBLANES)
        pltpu.make_async_copy(self.src.at[rows], self.buf.at[slot, rows], self.sems.at[slot]).wait()

    def begin(self, step, wait_rows=True):
        @pl.when(step == 0)
        def _():
            spare = (self.slots, SUBLANES, LANES)
            self.buf[:, self.n * SUBLANES:, :] = jnp.zeros(spare, self.buf.dtype)
            self.pace_smem[0] = jnp.int32(0)
            for tile in range(self.ahead):
                self._idx_copy(tile).start()
                self._idx_copy(tile).wait()
                self._start(tile)
            self._idx_copy(self.ahead).start()

        self._idx_copy(step + self.ahead + 1).start()
        self._idx_copy(step + self.ahead).wait()
        if wait_rows:
            self._wait(step)

    def fetch_group(self, step, part, parts=DMA_GROUPS):
        tile = step + self.ahead
        self._start(tile, part, parts, paced=True)
        spare = self.buf[lax.rem(tile + self.slots, self.slots), self.n * SUBLANES:, :]
        bits = spare if spare.dtype == U32 else lax.bitcast_convert_type(spare, U32)
        return ((bits >> 16) >> 16)[0:1, 0:1]

    def fetch_all_then_wait(self, step):
        self._start(step + self.ahead)
        self._wait(step)

    def rows(self, step):
        return self.buf.at[lax.rem(step, self.slots)]

    def end(self, step, n_steps):
        @pl.when(step == n_steps - 1)
        def _():
            for k in range(1, self.ahead + 1):
                self._wait(step + k)
            self._idx_copy(step + self.ahead + 1).wait()

    @staticmethod
    def scratch(slots, n, dtype):
        return [pltpu.VMEM((slots, (n + 1) * SUBLANES, LANES), dtype),
                pltpu.SemaphoreType.DMA((slots,)),
                pltpu.SMEM((_RowGather.IDX_SLOTS, 1, n), jnp.int32),
                pltpu.SemaphoreType.DMA((_RowGather.IDX_SLOTS,)),
                pltpu.SMEM((1,), jnp.int32)]


def _zero_f32(zero_bits):
    return lax.bitcast_convert_type(zero_bits, F32)


def _linear_step():
    return (pl.program_id(0) * pl.num_programs(1) + pl.program_id(1),
            pl.num_programs(0) * pl.num_programs(1))


def _ada_kernel(c_ref, w_ref, b_ref, o_ref):
    c = c_ref[...]
    c_act = (c * jax.nn.sigmoid(c)).astype(BF16)
    o_ref[0] = jnp.dot(c_act, w_ref[0].astype(BF16), preferred_element_type=F32) + b_ref[0]


def _ada(c, w_ada, b_ada):
    depth = w_ada.shape[0]
    batch = c.shape[0]
    n_col = w_ada.shape[2] // D_MODEL
    return pl.pallas_call(
        _ada_kernel,
        out_shape=jax.ShapeDtypeStruct((depth, batch, n_col * D_MODEL), F32),
        grid=(depth, n_col),
        in_specs=[
            pl.BlockSpec((batch, D_MODEL), lambda l, j: (0, 0)),
            pl.BlockSpec((1, D_MODEL, D_MODEL), lambda l, j: (l, 0, j)),
            pl.BlockSpec((1, 1, D_MODEL), lambda l, j: (l, 0, j)),
        ],
        out_specs=pl.BlockSpec((1, batch, D_MODEL), lambda l, j: (l, 0, j)),
        compiler_params=pltpu.CompilerParams(
            dimension_semantics=("arbitrary", "arbitrary"), vmem_limit_bytes=VMEM_LIMIT),
        name="ada",
    )(c, w_ada, b_ada.reshape(depth, 1, -1))


def _route(logits_t, bias):
    s = jax.nn.sigmoid(logits_t)
    sel = s + bias
    sel_r = [sel[e:e + 1] for e in range(N_EXPERTS)]
    s_r = [s[e:e + 1] for e in range(N_EXPERTS)]

    def group_score(g):
        r = sel_r[GROUP_SIZE * g:GROUP_SIZE * (g + 1)]
        best = r[PAIR_A[0]] + r[PAIR_B[0]]
        for p in range(1, N_PAIRS):
            best = jnp.maximum(best, r[PAIR_A[p]] + r[PAIR_B[p]])
        return best

    best_v = group_score(0)
    best_g = jnp.zeros_like(best_v)
    for g in range(1, N_GROUPS):
        gs = group_score(g)
        upd = gs > best_v
        best_v = jnp.where(upd, gs, best_v)
        best_g = jnp.where(upd, float(g), best_g)

    def pick(rows, i):
        out = rows[i]
        for g in range(1, N_GROUPS):
            out = jnp.where(best_g == float(g), rows[GROUP_SIZE * g + i], out)
        return out

    selg = [pick(sel_r, i) for i in range(GROUP_SIZE)]
    sg = [pick(s_r, i) for i in range(GROUP_SIZE)]
    chosen = []
    for i in range(GROUP_SIZE):
        beaten = jnp.zeros_like(best_v)
        for j in range(GROUP_SIZE):
            if j == i:
                continue
            wins = selg[j] > selg[i]
            if j < i:
                wins = wins | (selg[j] == selg[i])
            beaten = beaten + wins.astype(F32)
        chosen.append(beaten < 2.0)
    m0, m1, m2, m3 = chosen
    pair = jnp.where(m0, jnp.where(m1, 0.0, jnp.where(m2, 1.0, 2.0)),
                     jnp.where(m1, jnp.where(m2, 3.0, 4.0), 5.0))
    s_a = jnp.where(m0, sg[0], jnp.where(m1, sg[1], sg[2]))
    s_b = jnp.where(m3, sg[3], jnp.where(m2, sg[2], sg[1]))
    tot = s_a + s_b
    cls = best_g * float(N_PAIRS) + pair
    zero = jnp.zeros_like(cls)
    return jnp.concatenate([cls, s_a / tot, s_b / tot] + [zero] * (SUBLANES - 3), axis=0)


def _mix_kernel(*refs, prev_moe):
    if prev_moe:
        x_ref, pos_hbm, y_hbm, pmod_ref, *refs = refs
        *refs, ybuf, ysem, idx_smem, idx_sems, pace_smem = refs
    else:
        x_ref, *refs = refs
    (mod_ref, n1_ref, n2_ref, win_ref, cw_ref, pw_ref, ps_ref, wout_ref, wr_ref, rb_ref,
     x1_ref, h_ref, ri_ref, gbuf, vbuf) = refs
    ts = x_ref.shape[1]
    st = pl.program_id(1)
    mod = mod_ref[0]
    sh1, sc1, g1, sh2, sc2 = (mod[i:i + 1] for i in range(5))
    cw = cw_ref[...]
    ps = ps_ref[...]
    x = x_ref[0]

    if prev_moe:
        step, n_steps = _linear_step()
        prev_rows = _RowGather(pos_hbm, y_hbm, ybuf, ysem, idx_smem, idx_sems, pace_smem, ts)
        prev_rows.begin(step)
        x = x + pmod_ref[0][5:6] * _rows_to_tokens(prev_rows.rows(step), ts)
        cw = cw + _zero_f32(prev_rows.fetch_group(step, 0))

    @pl.when(st == 0)
    def _():
        gbuf[0:CONV_HALO] = jnp.zeros((CONV_HALO, CONV_WIDTH), F32)
        vbuf[0:POOL_HALO] = jnp.zeros((POOL_HALO, POOL_WIDTH), F32)

    h = _norm_mod(x, n1_ref[...], sh1, sc1).astype(BF16)
    proj = jnp.dot(h, win_ref[...], preferred_element_type=F32)
    u_b = proj[:, 0:CONV_WIDTH]
    u_c = proj[:, CONV_WIDTH:2 * CONV_WIDTH]
    u_h = proj[:, 2 * CONV_WIDTH:3 * CONV_WIDTH]
    v = proj[:, 3 * CONV_WIDTH:]
    if prev_moe:
        ps = ps + _zero_f32(prev_rows.fetch_group(step, 1))

    g = u_c * u_h
    gbuf[CONV_HALO:CONV_HALO + ts] = g
    vbuf[POOL_HALO:POOL_HALO + ts] = v

    conv = (cw[0:1] * gbuf[CONV_HALO - 2:CONV_HALO - 2 + ts]
            + cw[1:2] * gbuf[CONV_HALO - 1:CONV_HALO - 1 + ts]
            + cw[2:3] * g)
    y_conv = u_b * conv

    t_pos = (st * ts + lax.broadcasted_iota(jnp.int32, (ts, 1), 0) + 1).astype(F32)
    pooled = []
    for gi, win in enumerate(POOL_WINDOWS):
        lo = gi * POOL_GROUP
        v_g = v[:, lo:lo + POOL_GROUP]
        acc = v_g
        for k in range(1, win):
            acc = acc + vbuf[POOL_HALO - k:POOL_HALO - k + ts, lo:lo + POOL_GROUP]
        inv_cnt = 1.0 / jnp.minimum(t_pos, float(win))
        pooled.append(acc * inv_cnt - v_g)
    pooled = jnp.concatenate(pooled, axis=1).astype(BF16)
    half = POOL_WIDTH // 2
    y_pool = jnp.concatenate(
        [jnp.dot(pooled[:, 0:half], pw_ref[0], preferred_element_type=F32),
         jnp.dot(pooled[:, half:], pw_ref[1], preferred_element_type=F32)], axis=1) * ps
    if prev_moe:
        g1 = g1 + _zero_f32(prev_rows.fetch_group(step, 2))

    gbuf[0:CONV_HALO] = gbuf[ts:ts + CONV_HALO]
    vbuf[0:POOL_HALO] = vbuf[ts:ts + POOL_HALO]

    y_mix = jnp.concatenate([y_conv, y_pool], axis=1).astype(BF16)
    x1 = x + g1 * jnp.dot(y_mix, wout_ref[...], preferred_element_type=F32)
    x1_ref[0] = x1
    if prev_moe:
        sh2 = sh2 + _zero_f32(prev_rows.fetch_group(step, 3))

    h2 = _norm_mod(x1, n2_ref[...], sh2, sc2)
    logits_t = lax.dot_general(wr_ref[...], h2.astype(BF16), (((1,), (1,)), ((), ())),
                               preferred_element_type=F32)
    info = _route(logits_t, rb_ref[...])
    ri_ref[...] = info

    packed = pltpu.pack_elementwise([h2[:, :D_MODEL // 2], h2[:, D_MODEL // 2:]], packed_dtype=BF16)
    for j in range(PACKED_SLABS):
        h_ref[pl.ds(j, ts, stride=SUBLANES), :] = packed[:, j * LANES:(j + 1) * LANES]
    info_t = jnp.concatenate([info, jnp.zeros((LANES - SUBLANES, ts), F32)], axis=0).T
    h_ref[pl.ds(INFO_SLAB, ts, stride=SUBLANES), :] = lax.bitcast_convert_type(info_t, U32)
    for j in range(INFO_SLAB + 1, SUBLANES):
        h_ref[pl.ds(j, ts, stride=SUBLANES), :] = jnp.zeros((ts, LANES), U32)

    if prev_moe:
        prev_rows.end(step, n_steps)


def _mix(x, prev, mod, n1, n2, win, cw, pw, ps, wout, wr, rb):
    batch, seq, _ = x.shape
    ts = MIX_TOKENS
    n_s = seq // ts
    tokens = batch * seq
    const = lambda shape: pl.BlockSpec(shape, lambda b, s: (0,) * len(shape))
    prev_specs = [] if prev is None else [
        pl.BlockSpec(memory_space=pl.ANY),
        pl.BlockSpec(memory_space=pl.ANY),
        pl.BlockSpec((1, 6, D_MODEL), lambda b, s: (b, 0, 0))]
    prev_scratch = [] if prev is None else _RowGather.scratch(TOKEN_SLOTS, ts, F32)
    return pl.pallas_call(
        functools.partial(_mix_kernel, prev_moe=prev is not None),
        out_shape=(
            jax.ShapeDtypeStruct((batch, seq, D_MODEL), F32),
            jax.ShapeDtypeStruct((tokens * SUBLANES, LANES), U32),
            jax.ShapeDtypeStruct((SUBLANES, tokens), F32),
        ),
        grid_spec=pltpu.PrefetchScalarGridSpec(
            num_scalar_prefetch=0,
            grid=(batch, n_s),
            in_specs=[
                pl.BlockSpec((1, ts, D_MODEL), lambda b, s: (b, s, 0)),
                *prev_specs,
                pl.BlockSpec((1, 6, D_MODEL), lambda b, s: (b, 0, 0)),
                const((1, D_MODEL)), const((1, D_MODEL)),
                const((D_MODEL, IN_COLS)),
                const((3, CONV_WIDTH)),
                const((2, POOL_WIDTH // 2, POOL_WIDTH // 2)),
                const((1, POOL_WIDTH)),
                const((D_MODEL, D_MODEL)),
                const((N_EXPERTS, D_MODEL)),
                const((N_EXPERTS, 1)),
            ],
            out_specs=(
                pl.BlockSpec((1, ts, D_MODEL), lambda b, s: (b, s, 0)),
                pl.BlockSpec((ts * SUBLANES, LANES), lambda b, s: (b * n_s + s, 0)),
                pl.BlockSpec((SUBLANES, ts), lambda b, s: (0, b * n_s + s)),
            ),
            scratch_shapes=[
                pltpu.VMEM((ts + CONV_HALO, CONV_WIDTH), F32),
                pltpu.VMEM((ts + POOL_HALO, POOL_WIDTH), F32),
                *prev_scratch,
            ],
        ),
        compiler_params=pltpu.CompilerParams(
            dimension_semantics=("arbitrary", "arbitrary"), vmem_limit_bytes=VMEM_LIMIT),
        name="mix",
    )(x, *(() if prev is None else prev), mod, n1, n2, win, cw, pw, ps, wout, wr, rb)


def _ffn_kernel(ea_ref, eb_ref, inv_hbm, h_hbm, wga_ref, wgb_ref, wda_ref, wdb_ref, y_ref,
                xbuf, sems, idx_smem, idx_sems, pace_smem):
    del ea_ref, eb_ref
    tm = FFN_ROWS
    i = pl.program_id(0)
    tok_rows = _RowGather(inv_hbm, h_hbm, xbuf, sems, idx_smem, idx_sems, pace_smem, tm)
    tok_rows.begin(i)
    rows = tok_rows.rows(i)
    slabs = [rows[pl.ds(j, tm, stride=SUBLANES), :] for j in range(INFO_SLAB + 1)]
    packed = jnp.concatenate(slabs[:PACKED_SLABS], axis=1)
    xg = jnp.concatenate(
        [pltpu.unpack_elementwise(packed, index=k, packed_dtype=BF16, unpacked_dtype=F32)
         for k in range(2)], axis=1).astype(BF16)
    info_a = lax.bitcast_convert_type(slabs[INFO_SLAB] | tok_rows.fetch_group(i, 0), F32)

    def act(wg_ref, w):
        gu = jnp.dot(xg, wg_ref[0], preferred_element_type=F32)
        gate = gu[:, :D_EXPERT]
        up = gu[:, D_EXPERT:]
        return (gate * jax.nn.sigmoid(gate) * up * w).astype(BF16)

    act_a = act(wga_ref, info_a[:, 1:2])
    info_b = lax.bitcast_convert_type(slabs[INFO_SLAB] | tok_rows.fetch_group(i, 1), F32)
    act_b = act(wgb_ref, info_b[:, 2:3])
    zero_lo = _zero_f32(tok_rows.fetch_group(i, 2))
    y = (jnp.dot(act_a, wda_ref[0], preferred_element_type=F32)
         + jnp.dot(act_b, wdb_ref[0], preferred_element_type=F32))
    zero_hi = _zero_f32(tok_rows.fetch_group(i, 3))
    for j in range(SUBLANES):
        zero = zero_lo if j < SUBLANES // 2 else zero_hi
        y_ref[pl.ds(j, tm, stride=SUBLANES), :] = y[:, j * LANES:(j + 1) * LANES] + zero
    tok_rows.end(i, pl.num_programs(0))


def _ffn(tile_ea, tile_eb, inv, h_rows, wgu, wd):
    n_tiles = tile_ea.shape[0]
    tm = FFN_ROWS
    return pl.pallas_call(
        _ffn_kernel,
        out_shape=jax.ShapeDtypeStruct((n_tiles * tm * SUBLANES, LANES), F32),
        grid_spec=pltpu.PrefetchScalarGridSpec(
            num_scalar_prefetch=2,
            grid=(n_tiles,),
            in_specs=[
                pl.BlockSpec(memory_space=pl.ANY),
                pl.BlockSpec(memory_space=pl.ANY),
                pl.BlockSpec((1, D_MODEL, 2 * D_EXPERT), lambda i, ea, eb: (ea[i], 0, 0)),
                pl.BlockSpec((1, D_MODEL, 2 * D_EXPERT), lambda i, ea, eb: (eb[i], 0, 0)),
                pl.BlockSpec((1, D_EXPERT, D_MODEL), lambda i, ea, eb: (ea[i], 0, 0)),
                pl.BlockSpec((1, D_EXPERT, D_MODEL), lambda i, ea, eb: (eb[i], 0, 0)),
            ],
            out_specs=pl.BlockSpec((tm * SUBLANES, LANES), lambda i, ea, eb: (i, 0)),
            scratch_shapes=_RowGather.scratch(FFN_SLOTS, tm, U32),
        ),
        compiler_params=pltpu.CompilerParams(
            dimension_semantics=("arbitrary",), vmem_limit_bytes=VMEM_LIMIT),
        name="ffn",
    )(tile_ea, tile_eb, inv, h_rows, wgu, wgu, wd, wd)


def _fin_kernel(x1_ref, mod_ref, fg_ref, pos_hbm, y_hbm, o_ref, ybuf, ysem, idx_smem, idx_sems,
                pace_smem):
    tf = x1_ref.shape[1]
    step, n_steps = _linear_step()
    last_rows = _RowGather(pos_hbm, y_hbm, ybuf, ysem, idx_smem, idx_sems, pace_smem, tf)
    last_rows.begin(step, wait_rows=False)
    last_rows.fetch_all_then_wait(step)
    x2 = x1_ref[0] + mod_ref[0][5:6] * _rows_to_tokens(last_rows.rows(step), tf)
    ms = jnp.mean(x2 * x2, axis=-1, keepdims=True)
    o_ref[0] = x2 * lax.rsqrt(ms + EPS) * fg_ref[...]
    last_rows.end(step, n_steps)


def _fin(pos, x1, y_rows, mod, final_g):
    batch, seq, _ = x1.shape
    tf = FIN_TOKENS
    return pl.pallas_call(
        _fin_kernel,
        out_shape=jax.ShapeDtypeStruct((batch, seq, D_MODEL), F32),
        grid=(batch, seq // tf),
        in_specs=[
            pl.BlockSpec((1, tf, D_MODEL), lambda b, s: (b, s, 0)),
            pl.BlockSpec((1, 6, D_MODEL), lambda b, s: (b, 0, 0)),
            pl.BlockSpec((1, D_MODEL), lambda b, s: (0, 0)),
            pl.BlockSpec(memory_space=pl.ANY),
            pl.BlockSpec(memory_space=pl.ANY),
        ],
        out_specs=pl.BlockSpec((1, tf, D_MODEL), lambda b, s: (b, s, 0)),
        scratch_shapes=_RowGather.scratch(TOKEN_SLOTS, tf, F32),
        compiler_params=pltpu.CompilerParams(
            dimension_semantics=("arbitrary", "arbitrary"), vmem_limit_bytes=VMEM_LIMIT),
        name="fin",
    )(x1, mod, final_g, pos, y_rows)


def _plan(cls, n_tiles):
    tm = FFN_ROWS
    tokens = cls.shape[0]
    onehot = (cls[:, None] == jnp.arange(N_CLASSES, dtype=jnp.int32)[None, :]).astype(jnp.int32)
    csum = jnp.cumsum(onehot, axis=0)
    rank = jnp.take_along_axis(csum, cls[:, None], axis=1)[:, 0] - 1
    counts = csum[-1]
    tiles = (counts + tm - 1) // tm
    tile_end = jnp.cumsum(tiles)
    tile_start = tile_end - tiles
    pos = (tile_start[cls] * tm + rank).astype(jnp.int32)
    j = jnp.arange(n_tiles, dtype=jnp.int32)
    total = tile_end[-1]
    j_eff = jnp.minimum(j, total - 1)
    tile_cls = jnp.sum((tile_end[None, :] <= j_eff[:, None]).astype(jnp.int32), axis=1)
    group = tile_cls // N_PAIRS
    pair = tile_cls % N_PAIRS
    ea = group * GROUP_SIZE + jnp.asarray(PAIR_A, jnp.int32)[pair]
    eb = group * GROUP_SIZE + jnp.asarray(PAIR_B, jnp.int32)[pair]
    inv = jnp.zeros(((n_tiles + FFN_SLOTS) * tm,), jnp.int32).at[pos].set(
        jnp.arange(tokens, dtype=jnp.int32), unique_indices=True)
    assert MIX_TOKENS == FIN_TOKENS
    pos_ahead = jnp.concatenate([pos, jnp.zeros((TOKEN_SLOTS * MIX_TOKENS,), jnp.int32)])
    return (pos_ahead.reshape(-1, 1, MIX_TOKENS), inv.reshape(-1, 1, tm), ea, eb)


def kernel(x, c, w_in, conv_w, pool_w, pool_scale, w_out, norm1_g, norm2_g, w_ada, b_ada,
           w_router, router_bias, w_gate_up, w_down, final_g):
    batch, seq, _ = x.shape
    depth = w_in.shape[0]
    tokens = batch * seq
    n_tiles = tokens // FFN_ROWS + N_CLASSES

    mod = _ada(c, w_ada, b_ada).reshape(depth, batch, 6, D_MODEL)
    wr_t = w_router.T.astype(BF16)
    rb = router_bias.reshape(N_EXPERTS, 1).astype(F32)
    fg = final_g.reshape(1, D_MODEL)
    z = jnp.zeros((depth, POOL_GROUP, POOL_GROUP), F32)
    pw_blocks = jnp.stack([
        jnp.concatenate([jnp.concatenate([pool_w[:, 2 * k], z], axis=2),
                         jnp.concatenate([z, pool_w[:, 2 * k + 1]], axis=2)], axis=1)
        for k in range(2)], axis=1).astype(BF16)

    prev = None
    for l in range(depth):
        x, h_rows, info = _mix(
            x, prev, mod[l], norm1_g[l].reshape(1, -1), norm2_g[l].reshape(1, -1),
            w_in[l].astype(BF16), conv_w[l], pw_blocks[l], pool_scale[l].reshape(1, -1),
            w_out[l].astype(BF16), wr_t, rb)
        pos, inv, ea, eb = _plan(info[0].astype(jnp.int32), n_tiles)
        y_rows = _ffn(ea, eb, inv, h_rows, w_gate_up[l].astype(BF16), w_down[l].astype(BF16))
        prev = (pos, y_rows, mod[l])
    return _fin(pos, x, y_rows, mod[depth - 1], fg)
```

```python
import functools

import jax
import jax.numpy as jnp
from jax import lax
from jax.experimental import pallas as pl
from jax.experimental.pallas import tpu as pltpu

D_MODEL = 1024
CONV_WIDTH = 512
POOL_WIDTH = 512
IN_COLS = 3 * CONV_WIDTH + POOL_WIDTH
POOL_WINDOWS = (2, 4, 8, 16)
POOL_GROUP = 128
N_EXPERTS = 16
N_GROUPS = 4
GROUP_SIZE = 4
D_EXPERT = 512
EPS = 1e-6

LANES = 128
SUBLANES = 8
PACKED_SLABS = D_MODEL // 2 // LANES
INFO_SLAB = PACKED_SLABS
CONV_HALO = SUBLANES
POOL_HALO = 2 * SUBLANES

PAIR_A = (0, 0, 0, 1, 1, 2)
PAIR_B = (1, 2, 3, 2, 3, 3)
N_PAIRS = len(PAIR_A)
N_CLASSES = N_GROUPS * N_PAIRS

MIX_TOKENS = 512
FFN_ROWS = 256
FIN_TOKENS = 512
GATHER_SLOTS = 3
SCATTER_SLOTS = 3
DMA_GROUPS = 4
LEAD_TILES = 3
TAIL_TILES = 2
VMEM_LIMIT = 56 * 1024 * 1024

F32 = jnp.float32
BF16 = jnp.bfloat16
U32 = jnp.uint32


def _norm_mod(v, gain, shift, scale):
    ms = jnp.mean(v * v, axis=-1, keepdims=True)
    return (v * lax.rsqrt(ms + EPS)) * (gain * (1.0 + scale)) + shift


def _ada_kernel(c_ref, w_ref, b_ref, o_ref):
    c = c_ref[...]
    c_act = (c * jax.nn.sigmoid(c)).astype(BF16)
    o_ref[0] = jnp.dot(c_act, w_ref[0].astype(BF16), preferred_element_type=F32) + b_ref[0]


def _ada(c, w_ada, b_ada):
    depth = w_ada.shape[0]
    batch = c.shape[0]
    n_col = w_ada.shape[2] // D_MODEL
    return pl.pallas_call(
        _ada_kernel,
        out_shape=jax.ShapeDtypeStruct((depth, batch, n_col * D_MODEL), F32),
        grid=(depth, n_col),
        in_specs=[
            pl.BlockSpec((batch, D_MODEL), lambda l, j: (0, 0)),
            pl.BlockSpec((1, D_MODEL, D_MODEL), lambda l, j: (l, 0, j)),
            pl.BlockSpec((1, 1, D_MODEL), lambda l, j: (l, 0, j)),
        ],
        out_specs=pl.BlockSpec((1, batch, D_MODEL), lambda l, j: (l, 0, j)),
        compiler_params=pltpu.CompilerParams(
            dimension_semantics=("arbitrary", "arbitrary"), vmem_limit_bytes=VMEM_LIMIT),
        name="ada",
    )(c, w_ada, b_ada.reshape(depth, 1, -1))


def _route(logits_t, bias):
    s = jax.nn.sigmoid(logits_t)
    sel = s + bias
    sel_r = [sel[e:e + 1] for e in range(N_EXPERTS)]
    s_r = [s[e:e + 1] for e in range(N_EXPERTS)]

    def group_score(g):
        r = sel_r[GROUP_SIZE * g:GROUP_SIZE * (g + 1)]
        best = r[PAIR_A[0]] + r[PAIR_B[0]]
        for p in range(1, N_PAIRS):
            best = jnp.maximum(best, r[PAIR_A[p]] + r[PAIR_B[p]])
        return best

    best_v = group_score(0)
    best_g = jnp.zeros_like(best_v)
    for g in range(1, N_GROUPS):
        gs = group_score(g)
        upd = gs > best_v
        best_v = jnp.where(upd, gs, best_v)
        best_g = jnp.where(upd, float(g), best_g)

    def pick(rows, i):
        out = rows[i]
        for g in range(1, N_GROUPS):
            out = jnp.where(best_g == float(g), rows[GROUP_SIZE * g + i], out)
        return out

    selg = [pick(sel_r, i) for i in range(GROUP_SIZE)]
    sg = [pick(s_r, i) for i in range(GROUP_SIZE)]
    chosen = []
    for i in range(GROUP_SIZE):
        beaten = jnp.zeros_like(best_v)
        for j in range(GROUP_SIZE):
            if j == i:
                continue
            wins = selg[j] > selg[i]
            if j < i:
                wins = wins | (selg[j] == selg[i])
            beaten = beaten + wins.astype(F32)
        chosen.append(beaten < 2.0)
    m0, m1, m2, m3 = chosen
    pair = jnp.where(m0, jnp.where(m1, 0.0, jnp.where(m2, 1.0, 2.0)),
                     jnp.where(m1, jnp.where(m2, 3.0, 4.0), 5.0))
    s_a = jnp.where(m0, sg[0], jnp.where(m1, sg[1], sg[2]))
    s_b = jnp.where(m3, sg[3], jnp.where(m2, sg[2], sg[1]))
    tot = s_a + s_b
    cls = best_g * float(N_PAIRS) + pair
    zero = jnp.zeros_like(cls)
    return jnp.concatenate([cls, s_a / tot, s_b / tot] + [zero] * (SUBLANES - 3), axis=0)


def _rows_to_tokens(rows_ref, n):
    return jnp.concatenate(
        [rows_ref[pl.ds(j, n, stride=SUBLANES), :] for j in range(SUBLANES)], axis=1)


def _mix_kernel(*refs, prev_moe):
    if prev_moe:
        x_ref, y_ref, pmod_ref, *refs = refs
    else:
        x_ref, *refs = refs
    (mod_ref, n1_ref, n2_ref, win_ref, cw_ref, pw_ref, ps_ref, wout_ref, wr_ref, rb_ref,
     x1_ref, h_ref, ri_ref, gbuf, vbuf) = refs
    ts = x_ref.shape[1]
    st = pl.program_id(1)
    x = x_ref[0]
    if prev_moe:
        x = x + pmod_ref[0][5:6] * _rows_to_tokens(y_ref, ts)
    mod = mod_ref[0]
    sh1, sc1, g1, sh2, sc2 = (mod[i:i + 1] for i in range(5))

    h = _norm_mod(x, n1_ref[...], sh1, sc1).astype(BF16)
    proj = jnp.dot(h, win_ref[...], preferred_element_type=F32)
    u_b = proj[:, 0:CONV_WIDTH]
    u_c = proj[:, CONV_WIDTH:2 * CONV_WIDTH]
    u_h = proj[:, 2 * CONV_WIDTH:3 * CONV_WIDTH]
    v = proj[:, 3 * CONV_WIDTH:]

    @pl.when(st == 0)
    def _():
        gbuf[0:CONV_HALO] = jnp.zeros((CONV_HALO, CONV_WIDTH), F32)
        vbuf[0:POOL_HALO] = jnp.zeros((POOL_HALO, POOL_WIDTH), F32)

    g = u_c * u_h
    gbuf[CONV_HALO:CONV_HALO + ts] = g
    vbuf[POOL_HALO:POOL_HALO + ts] = v

    cw = cw_ref[...]
    conv = (cw[0:1] * gbuf[CONV_HALO - 2:CONV_HALO - 2 + ts]
            + cw[1:2] * gbuf[CONV_HALO - 1:CONV_HALO - 1 + ts]
            + cw[2:3] * g)
    y_conv = u_b * conv

    t_pos = (st * ts + lax.broadcasted_iota(jnp.int32, (ts, 1), 0) + 1).astype(F32)
    pooled = []
    for gi, win in enumerate(POOL_WINDOWS):
        lo = gi * POOL_GROUP
        v_g = v[:, lo:lo + POOL_GROUP]
        acc = vbuf[0:POOL_HALO + ts, lo:lo + POOL_GROUP]
        k = 1
        while k < win:
            acc = acc[k:] + acc[:-k]
            k *= 2
        first = POOL_HALO - win + 1
        inv_cnt = 1.0 / jnp.minimum(t_pos, float(win))
        pooled.append(acc[first:first + ts] * inv_cnt - v_g)
    pooled = jnp.concatenate(pooled, axis=1).astype(BF16)
    half = POOL_WIDTH // 2
    y_pool = jnp.concatenate(
        [jnp.dot(pooled[:, 0:half], pw_ref[0], preferred_element_type=F32),
         jnp.dot(pooled[:, half:], pw_ref[1], preferred_element_type=F32)], axis=1) * ps_ref[...]

    gbuf[0:CONV_HALO] = gbuf[ts:ts + CONV_HALO]
    vbuf[0:POOL_HALO] = vbuf[ts:ts + POOL_HALO]

    y_mix = jnp.concatenate([y_conv, y_pool], axis=1).astype(BF16)
    x1 = x + g1 * jnp.dot(y_mix, wout_ref[...], preferred_element_type=F32)
    x1_ref[0] = x1

    h2 = _norm_mod(x1, n2_ref[...], sh2, sc2)
    logits_t = lax.dot_general(wr_ref[...], h2.astype(BF16), (((1,), (1,)), ((), ())),
                               preferred_element_type=F32)
    info = _route(logits_t, rb_ref[...])
    ri_ref[...] = info

    packed = pltpu.pack_elementwise([h2[:, :D_MODEL // 2], h2[:, D_MODEL // 2:]], packed_dtype=BF16)
    for j in range(PACKED_SLABS):
        h_ref[pl.ds(j, ts, stride=SUBLANES), :] = packed[:, j * LANES:(j + 1) * LANES]
    info_t = jnp.concatenate([info, jnp.zeros((LANES - SUBLANES, ts), F32)], axis=0).T
    h_ref[pl.ds(INFO_SLAB, ts, stride=SUBLANES), :] = lax.bitcast_convert_type(info_t, U32)
    for j in range(INFO_SLAB + 1, SUBLANES):
        h_ref[pl.ds(j, ts, stride=SUBLANES), :] = jnp.zeros((ts, LANES), U32)


def _mix(x, prev, mod, n1, n2, win, cw, pw, ps, wout, wr, rb):
    batch, seq, _ = x.shape
    ts = MIX_TOKENS
    n_s = seq // ts
    tokens = batch * seq
    const = lambda shape: pl.BlockSpec(shape, lambda b, s: (0,) * len(shape))
    prev_specs = [] if prev is None else [
        pl.BlockSpec((ts * SUBLANES, LANES), lambda b, s: (b * n_s + s, 0)),
        pl.BlockSpec((1, 6, D_MODEL), lambda b, s: (b, 0, 0))]
    return pl.pallas_call(
        functools.partial(_mix_kernel, prev_moe=prev is not None),
        out_shape=(
            jax.ShapeDtypeStruct((batch, seq, D_MODEL), F32),
            jax.ShapeDtypeStruct((tokens * SUBLANES, LANES), U32),
            jax.ShapeDtypeStruct((SUBLANES, tokens), F32),
        ),
        grid=(batch, n_s),
        in_specs=[
            pl.BlockSpec((1, ts, D_MODEL), lambda b, s: (b, s, 0)),
            *prev_specs,
            pl.BlockSpec((1, 6, D_MODEL), lambda b, s: (b, 0, 0)),
            const((1, D_MODEL)), const((1, D_MODEL)),
            const((D_MODEL, IN_COLS)),
            const((3, CONV_WIDTH)),
            const((2, POOL_WIDTH // 2, POOL_WIDTH // 2)),
            const((1, POOL_WIDTH)),
            const((D_MODEL, D_MODEL)),
            const((N_EXPERTS, D_MODEL)),
            const((N_EXPERTS, 1)),
        ],
        out_specs=(
            pl.BlockSpec((1, ts, D_MODEL), lambda b, s: (b, s, 0)),
            pl.BlockSpec((ts * SUBLANES, LANES), lambda b, s: (b * n_s + s, 0)),
            pl.BlockSpec((SUBLANES, ts), lambda b, s: (0, b * n_s + s)),
        ),
        scratch_shapes=[
            pltpu.VMEM((ts + CONV_HALO, CONV_WIDTH), F32),
            pltpu.VMEM((ts + POOL_HALO, POOL_WIDTH), F32),
        ],
        compiler_params=pltpu.CompilerParams(
            dimension_semantics=("arbitrary", "arbitrary"), vmem_limit_bytes=VMEM_LIMIT),
        name="mix",
    )(x, *(() if prev is None else prev), mod, n1, n2, win, cw, pw, ps, wout, wr, rb)


def _row_copy(src, src_row, dst, dst_row, sem):
    return pltpu.make_async_copy(
        src.at[pl.ds(pl.multiple_of(src_row * SUBLANES, SUBLANES), SUBLANES)],
        dst.at[pl.ds(pl.multiple_of(dst_row * SUBLANES, SUBLANES), SUBLANES)], sem)


def _ffn_kernel(ea_ref, eb_ref, dst_ref, h_hbm, wga_ref, wgb_ref, wda_ref, wdb_ref, y_hbm,
                xbuf, ybuf, gsem, ssem, *, tokens):
    del ea_ref, eb_ref
    tm = FFN_ROWS
    i = pl.program_id(0)
    n = pl.num_programs(0)

    def dst_row(tile, r):
        return dst_ref[(tile + LEAD_TILES) * tm + r]

    def part_rows(part, parts):
        return range(part * tm // parts, (part + 1) * tm // parts)

    def gather(tile, part=0, parts=1):
        slot = lax.rem(tile, GATHER_SLOTS)
        for r in part_rows(part, parts):
            src = jnp.minimum(dst_row(tile, r), tokens - 1)
            _row_copy(h_hbm, src, xbuf.at[slot], r, gsem.at[slot]).start()

    def scatter(tile, part=0, parts=1):
        slot = lax.rem(tile + LEAD_TILES, SCATTER_SLOTS)
        for r in part_rows(part, parts):
            _row_copy(ybuf.at[slot], r, y_hbm, dst_row(tile, r), ssem.at[slot]).start()

    def wait_gather(tile):
        slot = lax.rem(tile, GATHER_SLOTS)
        pltpu.make_async_copy(h_hbm.at[pl.ds(0, tm * SUBLANES)],
                              xbuf.at[slot, pl.ds(0, tm * SUBLANES)], gsem.at[slot]).wait()

    def wait_scatter(tile):
        slot = lax.rem(tile + LEAD_TILES, SCATTER_SLOTS)
        pltpu.make_async_copy(ybuf.at[slot], y_hbm.at[pl.ds(0, tm * SUBLANES)], ssem.at[slot]).wait()

    @pl.when(i == 0)
    def _():
        ybuf[...] = jnp.zeros(ybuf.shape, F32)
        xbuf[:, tm * SUBLANES:, :] = jnp.zeros((GATHER_SLOTS, SUBLANES, LANES), U32)
        gather(0)
        gather(1)
        scatter(-3)
        scatter(-2)

    def move_rows(part):
        scatter(i - 1, part, DMA_GROUPS)
        gather(i + 2, part, DMA_GROUPS)
        spare = xbuf[lax.rem(i + 2 + GATHER_SLOTS, GATHER_SLOTS), tm * SUBLANES:, :]
        return ((spare >> 16) >> 16)[0:1]

    wait_scatter(i - 3)
    wait_gather(i)
    rows = xbuf.at[lax.rem(i, GATHER_SLOTS)]
    slabs = [rows[pl.ds(j, tm, stride=SUBLANES), :] for j in range(INFO_SLAB + 1)]
    packed = jnp.concatenate(slabs[:PACKED_SLABS], axis=1)
    xg = jnp.concatenate(
        [pltpu.unpack_elementwise(packed, index=k, packed_dtype=BF16, unpacked_dtype=F32)
         for k in range(2)], axis=1).astype(BF16)
    info_a = lax.bitcast_convert_type(slabs[INFO_SLAB] | move_rows(0), F32)

    def act(wg_ref, w):
        gu = jnp.dot(xg, wg_ref[0], preferred_element_type=F32)
        gate = gu[:, :D_EXPERT]
        up = gu[:, D_EXPERT:]
        return (gate * jax.nn.sigmoid(gate) * up * w).astype(BF16)

    act_a = act(wga_ref, info_a[:, 1:2])
    info_b = lax.bitcast_convert_type(slabs[INFO_SLAB] | move_rows(1), F32)
    act_b = act(wgb_ref, info_b[:, 2:3])
    zero_lo = lax.bitcast_convert_type(move_rows(2), F32)
    y = (jnp.dot(act_a, wda_ref[0], preferred_element_type=F32)
         + jnp.dot(act_b, wdb_ref[0], preferred_element_type=F32))
    zero_hi = lax.bitcast_convert_type(move_rows(3), F32)
    out = ybuf.at[lax.rem(i, SCATTER_SLOTS)]
    for j in range(SUBLANES):
        zero = zero_lo if j < SUBLANES // 2 else zero_hi
        out[pl.ds(j, tm, stride=SUBLANES), :] = y[:, j * LANES:(j + 1) * LANES] + zero

    @pl.when(i == n - 1)
    def _():
        scatter(i)
        wait_scatter(i - 2)
        wait_scatter(i - 1)
        wait_scatter(i)
        wait_gather(i + 1)
        wait_gather(i + 2)


def _ffn(tile_ea, tile_eb, dst, h_rows, wgu, wd):
    n_tiles = tile_ea.shape[0]
    tm = FFN_ROWS
    tokens = h_rows.shape[0] // SUBLANES
    return pl.pallas_call(
        functools.partial(_ffn_kernel, tokens=tokens),
        out_shape=jax.ShapeDtypeStruct(
            ((tokens + SCATTER_SLOTS * tm) * SUBLANES, LANES), F32),
        grid_spec=pltpu.PrefetchScalarGridSpec(
            num_scalar_prefetch=3,
            grid=(n_tiles,),
            in_specs=[
                pl.BlockSpec(memory_space=pl.ANY),
                pl.BlockSpec((1, D_MODEL, 2 * D_EXPERT), lambda i, ea, eb, ds: (ea[i], 0, 0)),
                pl.BlockSpec((1, D_MODEL, 2 * D_EXPERT), lambda i, ea, eb, ds: (eb[i], 0, 0)),
                pl.BlockSpec((1, D_EXPERT, D_MODEL), lambda i, ea, eb, ds: (ea[i], 0, 0)),
                pl.BlockSpec((1, D_EXPERT, D_MODEL), lambda i, ea, eb, ds: (eb[i], 0, 0)),
            ],
            out_specs=pl.BlockSpec(memory_space=pl.ANY),
            scratch_shapes=[pltpu.VMEM((GATHER_SLOTS, (tm + 1) * SUBLANES, LANES), U32),
                            pltpu.VMEM((SCATTER_SLOTS, tm * SUBLANES, LANES), F32),
                            pltpu.SemaphoreType.DMA((GATHER_SLOTS,)),
                            pltpu.SemaphoreType.DMA((SCATTER_SLOTS,))],
        ),
        compiler_params=pltpu.CompilerParams(
            dimension_semantics=("arbitrary",), vmem_limit_bytes=VMEM_LIMIT),
        name="ffn",
    )(tile_ea, tile_eb, dst, h_rows, wgu, wgu, wd, wd)


def _fin_kernel(x1_ref, y_ref, mod_ref, fg_ref, o_ref):
    x2 = x1_ref[0] + mod_ref[0][5:6] * _rows_to_tokens(y_ref, x1_ref.shape[1])
    ms = jnp.mean(x2 * x2, axis=-1, keepdims=True)
    o_ref[0] = x2 * lax.rsqrt(ms + EPS) * fg_ref[...]


def _fin(x1, y_rows, mod, final_g):
    batch, seq, _ = x1.shape
    tf = FIN_TOKENS
    n_s = seq // tf
    return pl.pallas_call(
        _fin_kernel,
        out_shape=jax.ShapeDtypeStruct((batch, seq, D_MODEL), F32),
        grid=(batch, n_s),
        in_specs=[
            pl.BlockSpec((1, tf, D_MODEL), lambda b, s: (b, s, 0)),
            pl.BlockSpec((tf * SUBLANES, LANES), lambda b, s: (b * n_s + s, 0)),
            pl.BlockSpec((1, 6, D_MODEL), lambda b, s: (b, 0, 0)),
            pl.BlockSpec((1, D_MODEL), lambda b, s: (0, 0)),
        ],
        out_specs=pl.BlockSpec((1, tf, D_MODEL), lambda b, s: (b, s, 0)),
        compiler_params=pltpu.CompilerParams(
            dimension_semantics=("arbitrary", "arbitrary"), vmem_limit_bytes=VMEM_LIMIT),
        name="fin",
    )(x1, y_rows, mod, final_g)


def _plan(cls, n_tiles):
    tm = FFN_ROWS
    onehot = (cls[:, None] == jnp.arange(N_CLASSES, dtype=jnp.int32)[None, :]).astype(jnp.int32)
    csum = jnp.cumsum(onehot, axis=0)
    rank = jnp.take_along_axis(csum, cls[:, None], axis=1)[:, 0] - 1
    counts = csum[-1]
    tiles = (counts + tm - 1) // tm
    tile_end = jnp.cumsum(tiles)
    tile_start = tile_end - tiles
    pos = tile_start[cls] * tm + rank
    j = jnp.arange(n_tiles, dtype=jnp.int32)
    total = tile_end[-1]
    j_eff = jnp.minimum(j, total - 1)
    tile_cls = jnp.sum((tile_end[None, :] <= j_eff[:, None]).astype(jnp.int32), axis=1)
    group = tile_cls // N_PAIRS
    pair = tile_cls % N_PAIRS
    ea = group * GROUP_SIZE + jnp.asarray(PAIR_A, jnp.int32)[pair]
    eb = group * GROUP_SIZE + jnp.asarray(PAIR_B, jnp.int32)[pair]
    tokens = cls.shape[0]
    rows = jnp.arange(-LEAD_TILES * tm, (n_tiles + TAIL_TILES) * tm, dtype=jnp.int32)
    dump = tokens + ((rows // tm) % SCATTER_SLOTS) * tm + rows % tm
    dst = dump.at[pos.astype(jnp.int32) + LEAD_TILES * tm].set(
        jnp.arange(tokens, dtype=jnp.int32), unique_indices=True)
    return dst, ea, eb


def kernel(x, c, w_in, conv_w, pool_w, pool_scale, w_out, norm1_g, norm2_g, w_ada, b_ada,
           w_router, router_bias, w_gate_up, w_down, final_g):
    batch, seq, _ = x.shape
    depth = w_in.shape[0]
    tokens = batch * seq
    n_tiles = tokens // FFN_ROWS + N_CLASSES

    mod = _ada(c, w_ada, b_ada).reshape(depth, batch, 6, D_MODEL)
    wr_t = w_router.T.astype(BF16)
    rb = router_bias.reshape(N_EXPERTS, 1).astype(F32)
    fg = final_g.reshape(1, D_MODEL)
    z = jnp.zeros((depth, POOL_GROUP, POOL_GROUP), F32)
    pw_blocks = jnp.stack([
        jnp.concatenate([jnp.concatenate([pool_w[:, 2 * k], z], axis=2),
                         jnp.concatenate([z, pool_w[:, 2 * k + 1]], axis=2)], axis=1)
        for k in range(2)], axis=1).astype(BF16)

    prev = None
    for l in range(depth):
        x, h_rows, info = _mix(
            x, prev, mod[l], norm1_g[l].reshape(1, -1), norm2_g[l].reshape(1, -1),
            w_in[l].astype(BF16), conv_w[l], pw_blocks[l], pool_scale[l].reshape(1, -1),
            w_out[l].astype(BF16), wr_t, rb)
        dst, ea, eb = _plan(info[0].astype(jnp.int32), n_tiles)
        y_rows = _ffn(ea, eb, dst, h_rows, w_gate_up[l].astype(BF16), w_down[l].astype(BF16))
        prev = (y_rows, mod[l])
    return _fin(x, y_rows, mod[depth - 1], fg)
```

```python
import functools

import jax
import jax.numpy as jnp
from jax import lax
from jax.experimental import pallas as pl
from jax.experimental.pallas import tpu as pltpu

D_MODEL = 1024
CONV_WIDTH = 512
POOL_WIDTH = 512
IN_COLS = 3 * CONV_WIDTH + POOL_WIDTH
POOL_WINDOWS = (2, 4, 8, 16)
POOL_GROUP = 128
N_EXPERTS = 16
N_GROUPS = 4
GROUP_SIZE = 4
D_EXPERT = 512
EPS = 1e-6

LANES = 128
SUBLANES = 8
PACKED_SLABS = D_MODEL // 2 // LANES
INFO_SLAB = PACKED_SLABS
CONV_HALO = SUBLANES
POOL_HALO = 2 * SUBLANES

PAIR_A = (0, 0, 0, 1, 1, 2)
PAIR_B = (1, 2, 3, 2, 3, 3)
N_PAIRS = len(PAIR_A)
N_CLASSES = N_GROUPS * N_PAIRS

MIX_TOKENS = 512
FFN_ROWS = 256
FIN_TOKENS = 512
DISPATCH_TOKENS = 1024
TOKEN_SLOTS = 2
DMA_GROUPS = 4
VMEM_LIMIT = 56 * 1024 * 1024

F32 = jnp.float32
BF16 = jnp.bfloat16
U32 = jnp.uint32


def _norm_mod(v, gain, shift, scale):
    ms = jnp.mean(v * v, axis=-1, keepdims=True)
    return (v * lax.rsqrt(ms + EPS)) * (gain * (1.0 + scale)) + shift


def _rows_to_tokens(rows_ref, n):
    return jnp.concatenate(
        [rows_ref[pl.ds(j, n, stride=SUBLANES), :] for j in range(SUBLANES)], axis=1)


def _row_copy(src, src_row, dst, dst_row, sem):
    return pltpu.make_async_copy(
        src.at[pl.ds(pl.multiple_of(src_row * SUBLANES, SUBLANES), SUBLANES)],
        dst.at[pl.ds(pl.multiple_of(dst_row * SUBLANES, SUBLANES), SUBLANES)], sem)


class _RowGather:
    def __init__(self, idx_ref, src_hbm, buf, sems, n):
        self.idx_ref, self.src, self.buf, self.sems, self.n = idx_ref, src_hbm, buf, sems, n
        self.slots = buf.shape[0]

    def zero_spare(self):
        spare = (self.slots, SUBLANES, LANES)
        self.buf[:, self.n * SUBLANES:, :] = jnp.zeros(spare, self.buf.dtype)

    def start(self, tile, part=0, parts=1):
        slot = lax.rem(tile, self.slots)
        for r in range(part * self.n // parts, (part + 1) * self.n // parts):
            _row_copy(self.src, self.idx_ref[tile * self.n + r], self.buf.at[slot], r,
                      self.sems.at[slot]).start()

    def start_anchored(self, tile, part):
        self.start(tile, part, DMA_GROUPS)
        spare = self.buf[lax.rem(tile + self.slots, self.slots), self.n * SUBLANES:, :]
        bits = lax.bitcast_convert_type(spare, U32)
        return lax.bitcast_convert_type(((bits >> 16) >> 16)[0:1, 0:1], F32)

    def wait(self, tile):
        slot = lax.rem(tile, self.slots)
        rows = pl.ds(0, self.n * SUBLANES)
        pltpu.make_async_copy(self.src.at[rows], self.buf.at[slot, rows], self.sems.at[slot]).wait()

    def rows(self, tile):
        return self.buf.at[lax.rem(tile, self.slots)]

    @staticmethod
    def scratch(n):
        return [pltpu.VMEM((TOKEN_SLOTS, (n + 1) * SUBLANES, LANES), F32),
                pltpu.SemaphoreType.DMA((TOKEN_SLOTS,))]


def _linear_step():
    return (pl.program_id(0) * pl.num_programs(1) + pl.program_id(1),
            pl.num_programs(0) * pl.num_programs(1))


def _ada_kernel(c_ref, w_ref, b_ref, o_ref):
    c = c_ref[...]
    c_act = (c * jax.nn.sigmoid(c)).astype(BF16)
    o_ref[0] = jnp.dot(c_act, w_ref[0].astype(BF16), preferred_element_type=F32) + b_ref[0]


def _ada(c, w_ada, b_ada):
    depth = w_ada.shape[0]
    batch = c.shape[0]
    n_col = w_ada.shape[2] // D_MODEL
    return pl.pallas_call(
        _ada_kernel,
        out_shape=jax.ShapeDtypeStruct((depth, batch, n_col * D_MODEL), F32),
        grid=(depth, n_col),
        in_specs=[
            pl.BlockSpec((batch, D_MODEL), lambda l, j: (0, 0)),
            pl.BlockSpec((1, D_MODEL, D_MODEL), lambda l, j: (l, 0, j)),
            pl.BlockSpec((1, 1, D_MODEL), lambda l, j: (l, 0, j)),
        ],
        out_specs=pl.BlockSpec((1, batch, D_MODEL), lambda l, j: (l, 0, j)),
        compiler_params=pltpu.CompilerParams(
            dimension_semantics=("arbitrary", "arbitrary"), vmem_limit_bytes=VMEM_LIMIT),
        name="ada",
    )(c, w_ada, b_ada.reshape(depth, 1, -1))


def _route(logits_t, bias):
    s = jax.nn.sigmoid(logits_t)
    sel = s + bias
    sel_r = [sel[e:e + 1] for e in range(N_EXPERTS)]
    s_r = [s[e:e + 1] for e in range(N_EXPERTS)]

    def group_score(g):
        r = sel_r[GROUP_SIZE * g:GROUP_SIZE * (g + 1)]
        best = r[PAIR_A[0]] + r[PAIR_B[0]]
        for p in range(1, N_PAIRS):
            best = jnp.maximum(best, r[PAIR_A[p]] + r[PAIR_B[p]])
        return best

    best_v = group_score(0)
    best_g = jnp.zeros_like(best_v)
    for g in range(1, N_GROUPS):
        gs = group_score(g)
        upd = gs > best_v
        best_v = jnp.where(upd, gs, best_v)
        best_g = jnp.where(upd, float(g), best_g)

    def pick(rows, i):
        out = rows[i]
        for g in range(1, N_GROUPS):
            out = jnp.where(best_g == float(g), rows[GROUP_SIZE * g + i], out)
        return out

    selg = [pick(sel_r, i) for i in range(GROUP_SIZE)]
    sg = [pick(s_r, i) for i in range(GROUP_SIZE)]
    chosen = []
    for i in range(GROUP_SIZE):
        beaten = jnp.zeros_like(best_v)
        for j in range(GROUP_SIZE):
            if j == i:
                continue
            wins = selg[j] > selg[i]
            if j < i:
                wins = wins | (selg[j] == selg[i])
            beaten = beaten + wins.astype(F32)
        chosen.append(beaten < 2.0)
    m0, m1, m2, m3 = chosen
    pair = jnp.where(m0, jnp.where(m1, 0.0, jnp.where(m2, 1.0, 2.0)),
                     jnp.where(m1, jnp.where(m2, 3.0, 4.0), 5.0))
    s_a = jnp.where(m0, sg[0], jnp.where(m1, sg[1], sg[2]))
    s_b = jnp.where(m3, sg[3], jnp.where(m2, sg[2], sg[1]))
    tot = s_a + s_b
    cls = best_g * float(N_PAIRS) + pair
    zero = jnp.zeros_like(cls)
    return jnp.concatenate([cls, s_a / tot, s_b / tot] + [zero] * (SUBLANES - 3), axis=0)


def _mix_kernel(*refs, prev_moe):
    if prev_moe:
        pos_ref, x_ref, y_hbm, pmod_ref, *refs = refs
        *refs, ybuf, ysem = refs
    else:
        x_ref, *refs = refs
    (mod_ref, n1_ref, n2_ref, win_ref, cw_ref, pw_ref, ps_ref, wout_ref, wr_ref, rb_ref,
     x1_ref, h_ref, ri_ref, gbuf, vbuf) = refs
    ts = x_ref.shape[1]
    st = pl.program_id(1)
    mod = mod_ref[0]
    sh1, sc1, g1, sh2, sc2 = (mod[i:i + 1] for i in range(5))
    cw = cw_ref[...]
    ps = ps_ref[...]
    x = x_ref[0]

    if prev_moe:
        step, n_steps = _linear_step()
        prev_rows = _RowGather(pos_ref, y_hbm, ybuf, ysem, ts)

        @pl.when(step == 0)
        def _():
            prev_rows.zero_spare()
            prev_rows.start(0)

        prev_rows.wait(step)
        x = x + pmod_ref[0][5:6] * _rows_to_tokens(prev_rows.rows(step), ts)
        cw = cw + prev_rows.start_anchored(step + 1, 0)

    @pl.when(st == 0)
    def _():
        gbuf[0:CONV_HALO] = jnp.zeros((CONV_HALO, CONV_WIDTH), F32)
        vbuf[0:POOL_HALO] = jnp.zeros((POOL_HALO, POOL_WIDTH), F32)

    h = _norm_mod(x, n1_ref[...], sh1, sc1).astype(BF16)
    proj = jnp.dot(h, win_ref[...], preferred_element_type=F32)
    u_b = proj[:, 0:CONV_WIDTH]
    u_c = proj[:, CONV_WIDTH:2 * CONV_WIDTH]
    u_h = proj[:, 2 * CONV_WIDTH:3 * CONV_WIDTH]
    v = proj[:, 3 * CONV_WIDTH:]
    if prev_moe:
        ps = ps + prev_rows.start_anchored(step + 1, 1)

    g = u_c * u_h
    gbuf[CONV_HALO:CONV_HALO + ts] = g
    vbuf[POOL_HALO:POOL_HALO + ts] = v

    conv = (cw[0:1] * gbuf[CONV_HALO - 2:CONV_HALO - 2 + ts]
            + cw[1:2] * gbuf[CONV_HALO - 1:CONV_HALO - 1 + ts]
            + cw[2:3] * g)
    y_conv = u_b * conv

    t_pos = (st * ts + lax.broadcasted_iota(jnp.int32, (ts, 1), 0) + 1).astype(F32)
    pooled = []
    for gi, win in enumerate(POOL_WINDOWS):
        lo = gi * POOL_GROUP
        v_g = v[:, lo:lo + POOL_GROUP]
        acc = vbuf[0:POOL_HALO + ts, lo:lo + POOL_GROUP]
        k = 1
        while k < win:
            acc = acc[k:] + acc[:-k]
            k *= 2
        first = POOL_HALO - win + 1
        inv_cnt = 1.0 / jnp.minimum(t_pos, float(win))
        pooled.append(acc[first:first + ts] * inv_cnt - v_g)
    pooled = jnp.concatenate(pooled, axis=1).astype(BF16)
    half = POOL_WIDTH // 2
    y_pool = jnp.concatenate(
        [jnp.dot(pooled[:, 0:half], pw_ref[0], preferred_element_type=F32),
         jnp.dot(pooled[:, half:], pw_ref[1], preferred_element_type=F32)], axis=1) * ps
    if prev_moe:
        g1 = g1 + prev_rows.start_anchored(step + 1, 2)

    gbuf[0:CONV_HALO] = gbuf[ts:ts + CONV_HALO]
    vbuf[0:POOL_HALO] = vbuf[ts:ts + POOL_HALO]

    y_mix = jnp.concatenate([y_conv, y_pool], axis=1).astype(BF16)
    x1 = x + g1 * jnp.dot(y_mix, wout_ref[...], preferred_element_type=F32)
    x1_ref[0] = x1
    if prev_moe:
        sh2 = sh2 + prev_rows.start_anchored(step + 1, 3)

    h2 = _norm_mod(x1, n2_ref[...], sh2, sc2)
    logits_t = lax.dot_general(wr_ref[...], h2.astype(BF16), (((1,), (1,)), ((), ())),
                               preferred_element_type=F32)
    info = _route(logits_t, rb_ref[...])
    ri_ref[...] = info

    packed = pltpu.pack_elementwise([h2[:, :D_MODEL // 2], h2[:, D_MODEL // 2:]], packed_dtype=BF16)
    for j in range(PACKED_SLABS):
        h_ref[pl.ds(j, ts, stride=SUBLANES), :] = packed[:, j * LANES:(j + 1) * LANES]
    info_t = jnp.concatenate([info, jnp.zeros((LANES - SUBLANES, ts), F32)], axis=0).T
    h_ref[pl.ds(INFO_SLAB, ts, stride=SUBLANES), :] = lax.bitcast_convert_type(info_t, U32)
    for j in range(INFO_SLAB + 1, SUBLANES):
        h_ref[pl.ds(j, ts, stride=SUBLANES), :] = jnp.zeros((ts, LANES), U32)

    if prev_moe:
        @pl.when(step == n_steps - 1)
        def _():
            prev_rows.wait(step + 1)


def _mix(x, prev, mod, n1, n2, win, cw, pw, ps, wout, wr, rb):
    batch, seq, _ = x.shape
    ts = MIX_TOKENS
    n_s = seq // ts
    tokens = batch * seq
    const = lambda shape: pl.BlockSpec(shape, lambda b, s, *_: (0,) * len(shape))
    prev_specs = [] if prev is None else [
        pl.BlockSpec(memory_space=pl.ANY),
        pl.BlockSpec((1, 6, D_MODEL), lambda b, s, *_: (b, 0, 0))]
    return pl.pallas_call(
        functools.partial(_mix_kernel, prev_moe=prev is not None),
        out_shape=(
            jax.ShapeDtypeStruct((batch, seq, D_MODEL), F32),
            jax.ShapeDtypeStruct((tokens * SUBLANES, LANES), U32),
            jax.ShapeDtypeStruct((SUBLANES, tokens), F32),
        ),
        grid_spec=pltpu.PrefetchScalarGridSpec(
            num_scalar_prefetch=0 if prev is None else 1,
            grid=(batch, n_s),
            in_specs=[
                pl.BlockSpec((1, ts, D_MODEL), lambda b, s, *_: (b, s, 0)),
                *prev_specs,
                pl.BlockSpec((1, 6, D_MODEL), lambda b, s, *_: (b, 0, 0)),
                const((1, D_MODEL)), const((1, D_MODEL)),
                const((D_MODEL, IN_COLS)),
                const((3, CONV_WIDTH)),
                const((2, POOL_WIDTH // 2, POOL_WIDTH // 2)),
                const((1, POOL_WIDTH)),
                const((D_MODEL, D_MODEL)),
                const((N_EXPERTS, D_MODEL)),
                const((N_EXPERTS, 1)),
            ],
            out_specs=(
                pl.BlockSpec((1, ts, D_MODEL), lambda b, s, *_: (b, s, 0)),
                pl.BlockSpec((ts * SUBLANES, LANES), lambda b, s, *_: (b * n_s + s, 0)),
                pl.BlockSpec((SUBLANES, ts), lambda b, s, *_: (0, b * n_s + s)),
            ),
            scratch_shapes=[
                pltpu.VMEM((ts + CONV_HALO, CONV_WIDTH), F32),
                pltpu.VMEM((ts + POOL_HALO, POOL_WIDTH), F32),
                *([] if prev is None else _RowGather.scratch(ts)),
            ],
        ),
        compiler_params=pltpu.CompilerParams(
            dimension_semantics=("arbitrary", "arbitrary"), vmem_limit_bytes=VMEM_LIMIT),
        name="mix",
    )(*(() if prev is None else prev[:1]), x, *(() if prev is None else prev[1:]),
      mod, n1, n2, win, cw, pw, ps, wout, wr, rb)


def _dispatch_kernel(pos_ref, h_ref, init_hbm, xs_hbm, sem):
    del init_hbm
    n = DISPATCH_TOKENS
    base = pl.program_id(0) * n

    def issue(r, carry):
        _row_copy(h_ref, r, xs_hbm, pos_ref[base + r], sem).start()
        return carry

    lax.fori_loop(0, n, issue, 0, unroll=8)
    pltpu.make_async_copy(h_ref, xs_hbm.at[pl.ds(0, n * SUBLANES)], sem).wait()


def _dispatch(pos, h_rows, n_rows):
    tokens = h_rows.shape[0] // SUBLANES
    init = jnp.zeros((n_rows * SUBLANES, LANES), U32)
    return pl.pallas_call(
        _dispatch_kernel,
        out_shape=jax.ShapeDtypeStruct((n_rows * SUBLANES, LANES), U32),
        grid_spec=pltpu.PrefetchScalarGridSpec(
            num_scalar_prefetch=1,
            grid=(tokens // DISPATCH_TOKENS,),
            in_specs=[pl.BlockSpec((DISPATCH_TOKENS * SUBLANES, LANES), lambda i, p: (i, 0)),
                      pl.BlockSpec(memory_space=pl.ANY)],
            out_specs=pl.BlockSpec(memory_space=pl.ANY),
            scratch_shapes=[pltpu.SemaphoreType.DMA],
        ),
        input_output_aliases={2: 0},
        compiler_params=pltpu.CompilerParams(
            dimension_semantics=("arbitrary",), vmem_limit_bytes=VMEM_LIMIT),
        name="dispatch",
    )(pos, h_rows, init)


def _ffn_kernel(ea_ref, eb_ref, valid_ref, xs_ref, wga_ref, wgb_ref, wda_ref, wdb_ref, y_ref):
    del ea_ref, eb_ref
    tm = FFN_ROWS
    i = pl.program_id(0)

    @pl.when(valid_ref[i] != 0)
    def _():
        slabs = [xs_ref[pl.ds(j, tm, stride=SUBLANES), :] for j in range(INFO_SLAB + 1)]
        packed = jnp.concatenate(slabs[:PACKED_SLABS], axis=1)
        xg = jnp.concatenate(
            [pltpu.unpack_elementwise(packed, index=k, packed_dtype=BF16, unpacked_dtype=F32)
             for k in range(2)], axis=1).astype(BF16)
        info = lax.bitcast_convert_type(slabs[INFO_SLAB], F32)

        def act(wg_ref, w):
            gu = jnp.dot(xg, wg_ref[0], preferred_element_type=F32)
            gate = gu[:, :D_EXPERT]
            up = gu[:, D_EXPERT:]
            return (gate * jax.nn.sigmoid(gate) * up * w).astype(BF16)

        y = (jnp.dot(act(wga_ref, info[:, 1:2]), wda_ref[0], preferred_element_type=F32)
             + jnp.dot(act(wgb_ref, info[:, 2:3]), wdb_ref[0], preferred_element_type=F32))
        for j in range(SUBLANES):
            y_ref[pl.ds(j, tm, stride=SUBLANES), :] = y[:, j * LANES:(j + 1) * LANES]

    @pl.when(valid_ref[i] == 0)
    def _():
        y_ref[...] = jnp.zeros(y_ref.shape, F32)


def _ffn(tile_ea, tile_eb, tile_valid, xs_rows, wgu, wd):
    n_tiles = tile_ea.shape[0]
    tm = FFN_ROWS
    return pl.pallas_call(
        _ffn_kernel,
        out_shape=jax.ShapeDtypeStruct((n_tiles * tm * SUBLANES, LANES), F32),
        grid_spec=pltpu.PrefetchScalarGridSpec(
            num_scalar_prefetch=3,
            grid=(n_tiles,),
            in_specs=[
                pl.BlockSpec((tm * SUBLANES, LANES), lambda i, ea, eb, va: (i, 0)),
                pl.BlockSpec((1, D_MODEL, 2 * D_EXPERT), lambda i, ea, eb, va: (ea[i], 0, 0)),
                pl.BlockSpec((1, D_MODEL, 2 * D_EXPERT), lambda i, ea, eb, va: (eb[i], 0, 0)),
                pl.BlockSpec((1, D_EXPERT, D_MODEL), lambda i, ea, eb, va: (ea[i], 0, 0)),
                pl.BlockSpec((1, D_EXPERT, D_MODEL), lambda i, ea, eb, va: (eb[i], 0, 0)),
            ],
            out_specs=pl.BlockSpec((tm * SUBLANES, LANES), lambda i, ea, eb, va: (i, 0)),
        ),
        compiler_params=pltpu.CompilerParams(
            dimension_semantics=("arbitrary",), vmem_limit_bytes=VMEM_LIMIT),
        name="ffn",
    )(tile_ea, tile_eb, tile_valid, xs_rows, wgu, wgu, wd, wd)


def _fin_kernel(pos_ref, x1_ref, mod_ref, fg_ref, y_hbm, o_ref, ybuf, ysem):
    tf = x1_ref.shape[1]
    step, n_steps = _linear_step()
    last_rows = _RowGather(pos_ref, y_hbm, ybuf, ysem, tf)

    @pl.when(step == 0)
    def _():
        last_rows.zero_spare()
        last_rows.start(0)

    last_rows.start(step + 1)
    last_rows.wait(step)
    x2 = x1_ref[0] + mod_ref[0][5:6] * _rows_to_tokens(last_rows.rows(step), tf)
    ms = jnp.mean(x2 * x2, axis=-1, keepdims=True)
    o_ref[0] = x2 * lax.rsqrt(ms + EPS) * fg_ref[...]

    @pl.when(step == n_steps - 1)
    def _():
        last_rows.wait(step + 1)


def _fin(pos, x1, y_rows, mod, final_g):
    batch, seq, _ = x1.shape
    tf = FIN_TOKENS
    return pl.pallas_call(
        _fin_kernel,
        out_shape=jax.ShapeDtypeStruct((batch, seq, D_MODEL), F32),
        grid_spec=pltpu.PrefetchScalarGridSpec(
            num_scalar_prefetch=1,
            grid=(batch, seq // tf),
            in_specs=[
                pl.BlockSpec((1, tf, D_MODEL), lambda b, s, p: (b, s, 0)),
                pl.BlockSpec((1, 6, D_MODEL), lambda b, s, p: (b, 0, 0)),
                pl.BlockSpec((1, D_MODEL), lambda b, s, p: (0, 0)),
                pl.BlockSpec(memory_space=pl.ANY),
            ],
            out_specs=pl.BlockSpec((1, tf, D_MODEL), lambda b, s, p: (b, s, 0)),
            scratch_shapes=_RowGather.scratch(tf),
        ),
        compiler_params=pltpu.CompilerParams(
            dimension_semantics=("arbitrary", "arbitrary"), vmem_limit_bytes=VMEM_LIMIT),
        name="fin",
    )(pos, x1, mod, final_g, y_rows)


def _plan(cls, n_tiles):
    tm = FFN_ROWS
    onehot = (cls[:, None] == jnp.arange(N_CLASSES, dtype=jnp.int32)[None, :]).astype(jnp.int32)
    csum = jnp.cumsum(onehot, axis=0)
    rank = jnp.take_along_axis(csum, cls[:, None], axis=1)[:, 0] - 1
    counts = csum[-1]
    tiles = (counts + tm - 1) // tm
    tile_end = jnp.cumsum(tiles)
    tile_start = tile_end - tiles
    pos = (tile_start[cls] * tm + rank).astype(jnp.int32)
    j = jnp.arange(n_tiles, dtype=jnp.int32)
    total = tile_end[-1]
    j_eff = jnp.minimum(j, total - 1)
    tile_cls = jnp.sum((tile_end[None, :] <= j_eff[:, None]).astype(jnp.int32), axis=1)
    group = tile_cls // N_PAIRS
    pair = tile_cls % N_PAIRS
    ea = group * GROUP_SIZE + jnp.asarray(PAIR_A, jnp.int32)[pair]
    eb = group * GROUP_SIZE + jnp.asarray(PAIR_B, jnp.int32)[pair]
    assert MIX_TOKENS == FIN_TOKENS
    pos_ahead = jnp.concatenate([pos, jnp.zeros((MIX_TOKENS,), jnp.int32)])
    return pos, pos_ahead, ea, eb, (j < total).astype(jnp.int32)


def kernel(x, c, w_in, conv_w, pool_w, pool_scale, w_out, norm1_g, norm2_g, w_ada, b_ada,
           w_router, router_bias, w_gate_up, w_down, final_g):
    batch, seq, _ = x.shape
    depth = w_in.shape[0]
    tokens = batch * seq
    n_tiles = tokens // FFN_ROWS + N_CLASSES

    mod = _ada(c, w_ada, b_ada).reshape(depth, batch, 6, D_MODEL)
    wr_t = w_router.T.astype(BF16)
    rb = router_bias.reshape(N_EXPERTS, 1).astype(F32)
    fg = final_g.reshape(1, D_MODEL)
    z = jnp.zeros((depth, POOL_GROUP, POOL_GROUP), F32)
    pw_blocks = jnp.stack([
        jnp.concatenate([jnp.concatenate([pool_w[:, 2 * k], z], axis=2),
                         jnp.concatenate([z, pool_w[:, 2 * k + 1]], axis=2)], axis=1)
        for k in range(2)], axis=1).astype(BF16)

    prev = None
    for l in range(depth):
        x, h_rows, info = _mix(
            x, prev, mod[l], norm1_g[l].reshape(1, -1), norm2_g[l].reshape(1, -1),
            w_in[l].astype(BF16), conv_w[l], pw_blocks[l], pool_scale[l].reshape(1, -1),
            w_out[l].astype(BF16), wr_t, rb)
        pos, pos_ahead, ea, eb, valid = _plan(info[0].astype(jnp.int32), n_tiles)
        xs_rows = _dispatch(pos, h_rows, n_tiles * FFN_ROWS)
        y_rows = _ffn(ea, eb, valid, xs_rows, w_gate_up[l].astype(BF16), w_down[l].astype(BF16))
        prev = (pos_ahead, y_rows, mod[l])
    return _fin(pos_ahead, x, y_rows, mod[depth - 1], fg)
```

```python
import functools

import jax
import jax.numpy as jnp
from jax import lax
from jax.experimental import pallas as pl
from jax.experimental.pallas import tpu as pltpu

D_MODEL = 1024
CONV_WIDTH = 512
POOL_WIDTH = 512
IN_COLS = 3 * CONV_WIDTH + POOL_WIDTH
POOL_WINDOWS = (2, 4, 8, 16)
POOL_GROUP = 128
N_EXPERTS = 16
N_GROUPS = 4
GROUP_SIZE = 4
D_EXPERT = 512
EPS = 1e-6

LANES = 128
SUBLANES = 8
PACKED_SLABS = D_MODEL // 2 // LANES
INFO_SLAB = PACKED_SLABS
CONV_HALO = SUBLANES
POOL_HALO = 2 * SUBLANES

PAIR_A = (0, 0, 0, 1, 1, 2)
PAIR_B = (1, 2, 3, 2, 3, 3)
N_PAIRS = len(PAIR_A)
N_CLASSES = N_GROUPS * N_PAIRS

MIX_TOKENS = 512
FFN_ROWS = 256
FIN_TOKENS = 512
DISPATCH_TOKENS = 1024
TOKEN_SLOTS = 2
DMA_GROUPS = 4
VMEM_LIMIT = 56 * 1024 * 1024

F32 = jnp.float32
BF16 = jnp.bfloat16
U32 = jnp.uint32


def _norm_mod(v, gain, shift, scale):
    ms = jnp.mean(v * v, axis=-1, keepdims=True)
    return (v * lax.rsqrt(ms + EPS)) * (gain * (1.0 + scale)) + shift


def _rows_to_tokens(rows_ref, n):
    return jnp.concatenate(
        [rows_ref[pl.ds(j, n, stride=SUBLANES), :] for j in range(SUBLANES)], axis=1)


def _row_copy(src, src_row, dst, dst_row, sem):
    return pltpu.make_async_copy(
        src.at[pl.ds(pl.multiple_of(src_row * SUBLANES, SUBLANES), SUBLANES)],
        dst.at[pl.ds(pl.multiple_of(dst_row * SUBLANES, SUBLANES), SUBLANES)], sem)


class _RowGather:
    def __init__(self, idx_ref, src_hbm, buf, sems, n):
        self.idx_ref, self.src, self.buf, self.sems, self.n = idx_ref, src_hbm, buf, sems, n
        self.slots = buf.shape[0]

    def zero_spare(self):
        spare = (self.slots, SUBLANES, LANES)
        self.buf[:, self.n * SUBLANES:, :] = jnp.zeros(spare, self.buf.dtype)

    def start(self, tile, part=0, parts=1):
        slot = lax.rem(tile, self.slots)
        for r in range(part * self.n // parts, (part + 1) * self.n // parts):
            _row_copy(self.src, self.idx_ref[tile * self.n + r], self.buf.at[slot], r,
                      self.sems.at[slot]).start()

    def start_anchored(self, tile, part):
        self.start(tile, part, DMA_GROUPS)
        spare = self.buf[lax.rem(tile + self.slots, self.slots), self.n * SUBLANES:, :]
        bits = lax.bitcast_convert_type(spare, U32)
        return lax.bitcast_convert_type(((bits >> 16) >> 16)[0:1, 0:1], F32)

    def wait(self, tile):
        slot = lax.rem(tile, self.slots)
        rows = pl.ds(0, self.n * SUBLANES)
        pltpu.make_async_copy(self.src.at[rows], self.buf.at[slot, rows], self.sems.at[slot]).wait()

    def rows(self, tile):
        return self.buf.at[lax.rem(tile, self.slots)]

    @staticmethod
    def scratch(n):
        return [pltpu.VMEM((TOKEN_SLOTS, (n + 1) * SUBLANES, LANES), F32),
                pltpu.SemaphoreType.DMA((TOKEN_SLOTS,))]


def _linear_step():
    return (pl.program_id(0) * pl.num_programs(1) + pl.program_id(1),
            pl.num_programs(0) * pl.num_programs(1))


def _ada_kernel(c_ref, w_ref, b_ref, o_ref):
    c = c_ref[...]
    c_act = (c * jax.nn.sigmoid(c)).astype(BF16)
    o_ref[0] = jnp.dot(c_act, w_ref[0].astype(BF16), preferred_element_type=F32) + b_ref[0]


def _ada(c, w_ada, b_ada):
    depth = w_ada.shape[0]
    batch = c.shape[0]
    n_col = w_ada.shape[2] // D_MODEL
    return pl.pallas_call(
        _ada_kernel,
        out_shape=jax.ShapeDtypeStruct((depth, batch, n_col * D_MODEL), F32),
        grid=(depth, n_col),
        in_specs=[
            pl.BlockSpec((batch, D_MODEL), lambda l, j: (0, 0)),
            pl.BlockSpec((1, D_MODEL, D_MODEL), lambda l, j: (l, 0, j)),
            pl.BlockSpec((1, 1, D_MODEL), lambda l, j: (l, 0, j)),
        ],
        out_specs=pl.BlockSpec((1, batch, D_MODEL), lambda l, j: (l, 0, j)),
        compiler_params=pltpu.CompilerParams(
            dimension_semantics=("arbitrary", "arbitrary"), vmem_limit_bytes=VMEM_LIMIT),
        name="ada",
    )(c, w_ada, b_ada.reshape(depth, 1, -1))


def _route(logits_t, bias):
    s = jax.nn.sigmoid(logits_t)
    sel = s + bias
    sel_r = [sel[e:e + 1] for e in range(N_EXPERTS)]
    s_r = [s[e:e + 1] for e in range(N_EXPERTS)]

    def group_score(g):
        r = sel_r[GROUP_SIZE * g:GROUP_SIZE * (g + 1)]
        best = r[PAIR_A[0]] + r[PAIR_B[0]]
        for p in range(1, N_PAIRS):
            best = jnp.maximum(best, r[PAIR_A[p]] + r[PAIR_B[p]])
        return best

    best_v = group_score(0)
    best_g = jnp.zeros_like(best_v)
    for g in range(1, N_GROUPS):
        gs = group_score(g)
        upd = gs > best_v
        best_v = jnp.where(upd, gs, best_v)
        best_g = jnp.where(upd, float(g), best_g)

    def pick(rows, i):
        out = rows[i]
        for g in range(1, N_GROUPS):
            out = jnp.where(best_g == float(g), rows[GROUP_SIZE * g + i], out)
        return out

    selg = [pick(sel_r, i) for i in range(GROUP_SIZE)]
    sg = [pick(s_r, i) for i in range(GROUP_SIZE)]
    chosen = []
    for i in range(GROUP_SIZE):
        beaten = jnp.zeros_like(best_v)
        for j in range(GROUP_SIZE):
            if j == i:
                continue
            wins = selg[j] > selg[i]
            if j < i:
                wins = wins | (selg[j] == selg[i])
            beaten = beaten + wins.astype(F32)
        chosen.append(beaten < 2.0)
    m0, m1, m2, m3 = chosen
    pair = jnp.where(m0, jnp.where(m1, 0.0, jnp.where(m2, 1.0, 2.0)),
                     jnp.where(m1, jnp.where(m2, 3.0, 4.0), 5.0))
    s_a = jnp.where(m0, sg[0], jnp.where(m1, sg[1], sg[2]))
    s_b = jnp.where(m3, sg[3], jnp.where(m2, sg[2], sg[1]))
    tot = s_a + s_b
    cls = best_g * float(N_PAIRS) + pair
    zero = jnp.zeros_like(cls)
    return jnp.concatenate([cls, s_a / tot, s_b / tot] + [zero] * (SUBLANES - 3), axis=0)


def _mix_kernel(*refs, prev_moe):
    if prev_moe:
        pos_ref, x_ref, y_hbm, pmod_ref, *refs = refs
        *refs, ybuf, ysem = refs
    else:
        x_ref, *refs = refs
    (mod_ref, n1_ref, n2_ref, win_ref, cw_ref, pw_ref, ps_ref, wout_ref, wr_ref, rb_ref,
     x1_ref, h_ref, ri_ref, gbuf, vbuf) = refs
    ts = x_ref.shape[1]
    st = pl.program_id(1)
    mod = mod_ref[0]
    sh1, sc1, g1, sh2, sc2 = (mod[i:i + 1] for i in range(5))
    cw = cw_ref[...]
    ps = ps_ref[...]
    x = x_ref[0]

    if prev_moe:
        step, n_steps = _linear_step()
        prev_rows = _RowGather(pos_ref, y_hbm, ybuf, ysem, ts)

        @pl.when(step == 0)
        def _():
            prev_rows.zero_spare()
            prev_rows.start(0)

        prev_rows.wait(step)
        x = x + pmod_ref[0][5:6] * _rows_to_tokens(prev_rows.rows(step), ts)
        cw = cw + prev_rows.start_anchored(step + 1, 0)

    @pl.when(st == 0)
    def _():
        gbuf[0:CONV_HALO] = jnp.zeros((CONV_HALO, CONV_WIDTH), F32)
        vbuf[0:POOL_HALO] = jnp.zeros((POOL_HALO, POOL_WIDTH), F32)

    h = _norm_mod(x, n1_ref[...], sh1, sc1).astype(BF16)
    proj = jnp.dot(h, win_ref[...], preferred_element_type=F32)
    u_b = proj[:, 0:CONV_WIDTH]
    u_c = proj[:, CONV_WIDTH:2 * CONV_WIDTH]
    u_h = proj[:, 2 * CONV_WIDTH:3 * CONV_WIDTH]
    v = proj[:, 3 * CONV_WIDTH:]
    if prev_moe:
        ps = ps + prev_rows.start_anchored(step + 1, 1)

    g = u_c * u_h
    gbuf[CONV_HALO:CONV_HALO + ts] = g
    vbuf[POOL_HALO:POOL_HALO + ts] = v

    conv = (cw[0:1] * gbuf[CONV_HALO - 2:CONV_HALO - 2 + ts]
            + cw[1:2] * gbuf[CONV_HALO - 1:CONV_HALO - 1 + ts]
            + cw[2:3] * g)
    y_conv = u_b * conv

    t_pos = (st * ts + lax.broadcasted_iota(jnp.int32, (ts, 1), 0) + 1).astype(F32)
    pooled = []
    for gi, win in enumerate(POOL_WINDOWS):
        lo = gi * POOL_GROUP
        v_g = v[:, lo:lo + POOL_GROUP]
        acc = vbuf[0:POOL_HALO + ts, lo:lo + POOL_GROUP]
        k = 1
        while k < win:
            acc = acc[k:] + acc[:-k]
            k *= 2
        first = POOL_HALO - win + 1
        inv_cnt = 1.0 / jnp.minimum(t_pos, float(win))
        pooled.append(acc[first:first + ts] * inv_cnt - v_g)
    pooled = jnp.concatenate(pooled, axis=1).astype(BF16)
    half = POOL_WIDTH // 2
    y_pool = jnp.concatenate(
        [jnp.dot(pooled[:, 0:half], pw_ref[0], preferred_element_type=F32),
         jnp.dot(pooled[:, half:], pw_ref[1], preferred_element_type=F32)], axis=1) * ps
    if prev_moe:
        g1 = g1 + prev_rows.start_anchored(step + 1, 2)

    gbuf[0:CONV_HALO] = gbuf[ts:ts + CONV_HALO]
    vbuf[0:POOL_HALO] = vbuf[ts:ts + POOL_HALO]

    y_mix = jnp.concatenate([y_conv, y_pool], axis=1).astype(BF16)
    x1 = x + g1 * jnp.dot(y_mix, wout_ref[...], preferred_element_type=F32)
    x1_ref[0] = x1
    if prev_moe:
        sh2 = sh2 + prev_rows.start_anchored(step + 1, 3)

    h2 = _norm_mod(x1, n2_ref[...], sh2, sc2)
    logits_t = lax.dot_general(wr_ref[...], h2.astype(BF16), (((1,), (1,)), ((), ())),
                               preferred_element_type=F32)
    info = _route(logits_t, rb_ref[...])
    ri_ref[...] = info

    packed = pltpu.pack_elementwise([h2[:, :D_MODEL // 2], h2[:, D_MODEL // 2:]], packed_dtype=BF16)
    for j in range(PACKED_SLABS):
        h_ref[pl.ds(j, ts, stride=SUBLANES), :] = packed[:, j * LANES:(j + 1) * LANES]
    info_t = jnp.concatenate([info, jnp.zeros((LANES - SUBLANES, ts), F32)], axis=0).T
    h_ref[pl.ds(INFO_SLAB, ts, stride=SUBLANES), :] = lax.bitcast_convert_type(info_t, U32)
    for j in range(INFO_SLAB + 1, SUBLANES):
        h_ref[pl.ds(j, ts, stride=SUBLANES), :] = jnp.zeros((ts, LANES), U32)

    if prev_moe:
        @pl.when(step == n_steps - 1)
        def _():
            prev_rows.wait(step + 1)


def _mix(x, prev, mod, n1, n2, win, cw, pw, ps, wout, wr, rb):
    batch, seq, _ = x.shape
    ts = MIX_TOKENS
    n_s = seq // ts
    tokens = batch * seq
    const = lambda shape: pl.BlockSpec(shape, lambda b, s, *_: (0,) * len(shape))
    prev_specs = [] if prev is None else [
        pl.BlockSpec(memory_space=pl.ANY),
        pl.BlockSpec((1, 6, D_MODEL), lambda b, s, *_: (b, 0, 0))]
    return pl.pallas_call(
        functools.partial(_mix_kernel, prev_moe=prev is not None),
        out_shape=(
            jax.ShapeDtypeStruct((batch, seq, D_MODEL), F32),
            jax.ShapeDtypeStruct((tokens * SUBLANES, LANES), U32),
            jax.ShapeDtypeStruct((SUBLANES, tokens), F32),
        ),
        grid_spec=pltpu.PrefetchScalarGridSpec(
            num_scalar_prefetch=0 if prev is None else 1,
            grid=(batch, n_s),
            in_specs=[
                pl.BlockSpec((1, ts, D_MODEL), lambda b, s, *_: (b, s, 0)),
                *prev_specs,
                pl.BlockSpec((1, 6, D_MODEL), lambda b, s, *_: (b, 0, 0)),
                const((1, D_MODEL)), const((1, D_MODEL)),
                const((D_MODEL, IN_COLS)),
                const((3, CONV_WIDTH)),
                const((2, POOL_WIDTH // 2, POOL_WIDTH // 2)),
                const((1, POOL_WIDTH)),
                const((D_MODEL, D_MODEL)),
                const((N_EXPERTS, D_MODEL)),
                const((N_EXPERTS, 1)),
            ],
            out_specs=(
                pl.BlockSpec((1, ts, D_MODEL), lambda b, s, *_: (b, s, 0)),
                pl.BlockSpec((ts * SUBLANES, LANES), lambda b, s, *_: (b * n_s + s, 0)),
                pl.BlockSpec((SUBLANES, ts), lambda b, s, *_: (0, b * n_s + s)),
            ),
            scratch_shapes=[
                pltpu.VMEM((ts + CONV_HALO, CONV_WIDTH), F32),
                pltpu.VMEM((ts + POOL_HALO, POOL_WIDTH), F32),
                *([] if prev is None else _RowGather.scratch(ts)),
            ],
        ),
        compiler_params=pltpu.CompilerParams(
            dimension_semantics=("arbitrary", "arbitrary"), vmem_limit_bytes=VMEM_LIMIT),
        name="mix",
    )(*(() if prev is None else prev[:1]), x, *(() if prev is None else prev[1:]),
      mod, n1, n2, win, cw, pw, ps, wout, wr, rb)


def _dispatch_kernel(pos_ref, h_ref, z_ref, xs_hbm, sem, *, token_steps):
    n = DISPATCH_TOKENS
    i = pl.program_id(0)
    base = i * n

    def scatter(src_ref):
        def issue(r, carry):
            _row_copy(src_ref, r, xs_hbm, pos_ref[base + r], sem).start()
            return carry

        lax.fori_loop(0, n, issue, 0, unroll=8)
        pltpu.make_async_copy(src_ref, xs_hbm.at[pl.ds(0, n * SUBLANES)], sem).wait()

    @pl.when(i < token_steps)
    def _():
        scatter(h_ref)

    @pl.when(i >= token_steps)
    def _():
        scatter(z_ref)


def _dispatch(pos_all, h_rows, n_rows):
    n = DISPATCH_TOKENS
    token_steps = h_rows.shape[0] // SUBLANES // n
    zero_rows = jnp.zeros((n * SUBLANES, LANES), U32)
    return pl.pallas_call(
        functools.partial(_dispatch_kernel, token_steps=token_steps),
        out_shape=jax.ShapeDtypeStruct((n_rows * SUBLANES, LANES), U32),
        grid_spec=pltpu.PrefetchScalarGridSpec(
            num_scalar_prefetch=1,
            grid=(pos_all.shape[0] // n,),
            in_specs=[pl.BlockSpec((n * SUBLANES, LANES),
                                   lambda i, p: (jnp.minimum(i, token_steps - 1), 0)),
                      pl.BlockSpec((n * SUBLANES, LANES), lambda i, p: (0, 0))],
            out_specs=pl.BlockSpec(memory_space=pl.ANY),
            scratch_shapes=[pltpu.SemaphoreType.DMA],
        ),
        compiler_params=pltpu.CompilerParams(
            dimension_semantics=("arbitrary",), vmem_limit_bytes=VMEM_LIMIT),
        name="dispatch",
    )(pos_all, h_rows, zero_rows)


def _ffn_kernel(ea_ref, eb_ref, valid_ref, xs_ref, wga_ref, wgb_ref, wda_ref, wdb_ref, y_ref):
    del ea_ref, eb_ref
    tm = FFN_ROWS
    i = pl.program_id(0)

    @pl.when(valid_ref[i] != 0)
    def _():
        slabs = [xs_ref[pl.ds(j, tm, stride=SUBLANES), :] for j in range(INFO_SLAB + 1)]
        packed = jnp.concatenate(slabs[:PACKED_SLABS], axis=1)
        xg = jnp.concatenate(
            [pltpu.unpack_elementwise(packed, index=k, packed_dtype=BF16, unpacked_dtype=F32)
             for k in range(2)], axis=1).astype(BF16)
        info = lax.bitcast_convert_type(slabs[INFO_SLAB], F32)

        def act(wg_ref, w):
            gu = jnp.dot(xg, wg_ref[0], preferred_element_type=F32)
            gate = gu[:, :D_EXPERT]
            up = gu[:, D_EXPERT:]
            return (gate * jax.nn.sigmoid(gate) * up * w).astype(BF16)

        y = (jnp.dot(act(wga_ref, info[:, 1:2]), wda_ref[0], preferred_element_type=F32)
             + jnp.dot(act(wgb_ref, info[:, 2:3]), wdb_ref[0], preferred_element_type=F32))
        for j in range(SUBLANES):
            y_ref[pl.ds(j, tm, stride=SUBLANES), :] = y[:, j * LANES:(j + 1) * LANES]

    @pl.when(valid_ref[i] == 0)
    def _():
        y_ref[...] = jnp.zeros(y_ref.shape, F32)


def _ffn(tile_ea, tile_eb, tile_valid, xs_rows, wgu, wd):
    n_tiles = tile_ea.shape[0]
    tm = FFN_ROWS
    return pl.pallas_call(
        _ffn_kernel,
        out_shape=jax.ShapeDtypeStruct((n_tiles * tm * SUBLANES, LANES), F32),
        grid_spec=pltpu.PrefetchScalarGridSpec(
            num_scalar_prefetch=3,
            grid=(n_tiles,),
            in_specs=[
                pl.BlockSpec((tm * SUBLANES, LANES), lambda i, ea, eb, va: (i, 0)),
                pl.BlockSpec((1, D_MODEL, 2 * D_EXPERT), lambda i, ea, eb, va: (ea[i], 0, 0)),
                pl.BlockSpec((1, D_MODEL, 2 * D_EXPERT), lambda i, ea, eb, va: (eb[i], 0, 0)),
                pl.BlockSpec((1, D_EXPERT, D_MODEL), lambda i, ea, eb, va: (ea[i], 0, 0)),
                pl.BlockSpec((1, D_EXPERT, D_MODEL), lambda i, ea, eb, va: (eb[i], 0, 0)),
            ],
            out_specs=pl.BlockSpec((tm * SUBLANES, LANES), lambda i, ea, eb, va: (i, 0)),
        ),
        compiler_params=pltpu.CompilerParams(
            dimension_semantics=("arbitrary",), vmem_limit_bytes=VMEM_LIMIT),
        name="ffn",
    )(tile_ea, tile_eb, tile_valid, xs_rows, wgu, wgu, wd, wd)


def _fin_kernel(pos_ref, x1_ref, mod_ref, fg_ref, y_hbm, o_ref, ybuf, ysem):
    tf = x1_ref.shape[1]
    step, n_steps = _linear_step()
    last_rows = _RowGather(pos_ref, y_hbm, ybuf, ysem, tf)

    @pl.when(step == 0)
    def _():
        last_rows.zero_spare()
        last_rows.start(0)

    last_rows.start(step + 1)
    last_rows.wait(step)
    x2 = x1_ref[0] + mod_ref[0][5:6] * _rows_to_tokens(last_rows.rows(step), tf)
    ms = jnp.mean(x2 * x2, axis=-1, keepdims=True)
    o_ref[0] = x2 * lax.rsqrt(ms + EPS) * fg_ref[...]

    @pl.when(step == n_steps - 1)
    def _():
        last_rows.wait(step + 1)


def _fin(pos, x1, y_rows, mod, final_g):
    batch, seq, _ = x1.shape
    tf = FIN_TOKENS
    return pl.pallas_call(
        _fin_kernel,
        out_shape=jax.ShapeDtypeStruct((batch, seq, D_MODEL), F32),
        grid_spec=pltpu.PrefetchScalarGridSpec(
            num_scalar_prefetch=1,
            grid=(batch, seq // tf),
            in_specs=[
                pl.BlockSpec((1, tf, D_MODEL), lambda b, s, p: (b, s, 0)),
                pl.BlockSpec((1, 6, D_MODEL), lambda b, s, p: (b, 0, 0)),
                pl.BlockSpec((1, D_MODEL), lambda b, s, p: (0, 0)),
                pl.BlockSpec(memory_space=pl.ANY),
            ],
            out_specs=pl.BlockSpec((1, tf, D_MODEL), lambda b, s, p: (b, s, 0)),
            scratch_shapes=_RowGather.scratch(tf),
        ),
        compiler_params=pltpu.CompilerParams(
            dimension_semantics=("arbitrary", "arbitrary"), vmem_limit_bytes=VMEM_LIMIT),
        name="fin",
    )(pos, x1, mod, final_g, y_rows)


def _plan(cls, n_tiles):
    tm = FFN_ROWS
    onehot = (cls[:, None] == jnp.arange(N_CLASSES, dtype=jnp.int32)[None, :]).astype(jnp.int32)
    csum = jnp.cumsum(onehot, axis=0)
    rank = jnp.take_along_axis(csum, cls[:, None], axis=1)[:, 0] - 1
    counts = csum[-1]
    tiles = (counts + tm - 1) // tm
    tile_end = jnp.cumsum(tiles)
    tile_start = tile_end - tiles
    pos = (tile_start[cls] * tm + rank).astype(jnp.int32)
    j = jnp.arange(n_tiles, dtype=jnp.int32)
    total = tile_end[-1]
    j_eff = jnp.minimum(j, total - 1)
    tile_cls = jnp.sum((tile_end[None, :] <= j_eff[:, None]).astype(jnp.int32), axis=1)
    group = tile_cls // N_PAIRS
    pair = tile_cls % N_PAIRS
    ea = group * GROUP_SIZE + jnp.asarray(PAIR_A, jnp.int32)[pair]
    eb = group * GROUP_SIZE + jnp.asarray(PAIR_B, jnp.int32)[pair]
    slot = jnp.arange(tm, dtype=jnp.int32)[None, :]
    is_pad = slot < (tiles * tm - counts)[:, None]
    spare_rank = (jnp.cumsum((~is_pad).reshape(-1).astype(jnp.int32)) - 1).reshape(N_CLASSES, tm)
    pad_pos = jnp.where(is_pad, (tile_start * tm + counts)[:, None] + slot, total * tm + spare_rank)
    pos_all = jnp.concatenate([pos, pad_pos.reshape(-1).astype(jnp.int32)])
    assert pos_all.shape[0] % DISPATCH_TOKENS == 0
    assert MIX_TOKENS == FIN_TOKENS
    pos_ahead = jnp.concatenate([pos, jnp.zeros((MIX_TOKENS,), jnp.int32)])
    return pos_all, pos_ahead, ea, eb, (j < total).astype(jnp.int32)


def kernel(x, c, w_in, conv_w, pool_w, pool_scale, w_out, norm1_g, norm2_g, w_ada, b_ada,
           w_router, router_bias, w_gate_up, w_down, final_g):
    batch, seq, _ = x.shape
    depth = w_in.shape[0]
    tokens = batch * seq
    n_tiles = tokens // FFN_ROWS + N_CLASSES

    mod = _ada(c, w_ada, b_ada).reshape(depth, batch, 6, D_MODEL)
    wr_t = w_router.T.astype(BF16)
    rb = router_bias.reshape(N_EXPERTS, 1).astype(F32)
    fg = final_g.reshape(1, D_MODEL)
    z = jnp.zeros((depth, POOL_GROUP, POOL_GROUP), F32)
    pw_blocks = jnp.stack([
        jnp.concatenate([jnp.concatenate([pool_w[:, 2 * k], z], axis=2),
                         jnp.concatenate([z, pool_w[:, 2 * k + 1]], axis=2)], axis=1)
        for k in range(2)], axis=1).astype(BF16)

    prev = None
    for l in range(depth):
        x, h_rows, info = _mix(
            x, prev, mod[l], norm1_g[l].reshape(1, -1), norm2_g[l].reshape(1, -1),
            w_in[l].astype(BF16), conv_w[l], pw_blocks[l], pool_scale[l].reshape(1, -1),
            w_out[l].astype(BF16), wr_t, rb)
        pos_all, pos_ahead, ea, eb, valid = _plan(info[0].astype(jnp.int32), n_tiles)
        xs_rows = _dispatch(pos_all, h_rows, n_tiles * FFN_ROWS)
        y_rows = _ffn(ea, eb, valid, xs_rows, w_gate_up[l].astype(BF16), w_down[l].astype(BF16))
        prev = (pos_ahead, y_rows, mod[l])
    return _fin(pos_ahead, x, y_rows, mod[depth - 1], fg)
```

```python
import functools

import jax
import jax.numpy as jnp
from jax import lax
from jax.experimental import pallas as pl
from jax.experimental.pallas import tpu as pltpu

D_MODEL = 1024
CONV_WIDTH = 512
POOL_WIDTH = 512
IN_COLS = 3 * CONV_WIDTH + POOL_WIDTH
POOL_WINDOWS = (2, 4, 8, 16)
POOL_GROUP = 128
N_EXPERTS = 16
N_GROUPS = 4
GROUP_SIZE = 4
D_EXPERT = 512
EPS = 1e-6

LANES = 128
SUBLANES = 8
PACKED_SLABS = D_MODEL // 2 // LANES
INFO_SLAB = PACKED_SLABS
CONV_HALO = SUBLANES
POOL_HALO = 2 * SUBLANES

PAIR_A = (0, 0, 0, 1, 1, 2)
PAIR_B = (1, 2, 3, 2, 3, 3)
N_PAIRS = len(PAIR_A)
N_CLASSES = N_GROUPS * N_PAIRS

MIX_TOKENS = 512
FFN_ROWS = 256
FIN_TOKENS = 512
DISPATCH_TOKENS = 1024
TOKEN_SLOTS = 2
DMA_GROUPS = 4
DMA_QUEUES = 2
VMEM_LIMIT = 56 * 1024 * 1024

F32 = jnp.float32
BF16 = jnp.bfloat16
U32 = jnp.uint32


def _norm_mod(v, gain, shift, scale):
    ms = jnp.mean(v * v, axis=-1, keepdims=True)
    return (v * lax.rsqrt(ms + EPS)) * (gain * (1.0 + scale)) + shift


def _rows_to_tokens(rows_ref, n):
    return jnp.concatenate(
        [rows_ref[pl.ds(j, n, stride=SUBLANES), :] for j in range(SUBLANES)], axis=1)


def _row_copy(src, src_row, dst, dst_row, sem):
    return pltpu.make_async_copy(
        src.at[pl.ds(pl.multiple_of(src_row * SUBLANES, SUBLANES), SUBLANES)],
        dst.at[pl.ds(pl.multiple_of(dst_row * SUBLANES, SUBLANES), SUBLANES)], sem)


class _RowGather:
    def __init__(self, idx_ref, src_hbm, buf, sems, n):
        self.idx_ref, self.src, self.buf, self.sems, self.n = idx_ref, src_hbm, buf, sems, n
        self.slots = buf.shape[0]

    def zero_spare(self):
        spare = (self.slots, SUBLANES, LANES)
        self.buf[:, self.n * SUBLANES:, :] = jnp.zeros(spare, self.buf.dtype)

    def start(self, tile, part=0, parts=1):
        slot = lax.rem(tile, self.slots)
        for r in range(part * self.n // parts, (part + 1) * self.n // parts):
            _row_copy(self.src, self.idx_ref[tile * self.n + r], self.buf.at[slot], r,
                      self.sems.at[slot]).start(priority=r % DMA_QUEUES)

    def start_anchored(self, tile, part):
        self.start(tile, part, DMA_GROUPS)
        spare = self.buf[lax.rem(tile + self.slots, self.slots), self.n * SUBLANES:, :]
        bits = lax.bitcast_convert_type(spare, U32)
        return lax.bitcast_convert_type(((bits >> 16) >> 16)[0:1, 0:1], F32)

    def wait(self, tile):
        slot = lax.rem(tile, self.slots)
        rows = pl.ds(0, self.n * SUBLANES)
        pltpu.make_async_copy(self.src.at[rows], self.buf.at[slot, rows], self.sems.at[slot]).wait()

    def rows(self, tile):
        return self.buf.at[lax.rem(tile, self.slots)]

    @staticmethod
    def scratch(n):
        return [pltpu.VMEM((TOKEN_SLOTS, (n + 1) * SUBLANES, LANES), F32),
                pltpu.SemaphoreType.DMA((TOKEN_SLOTS,))]


def _linear_step():
    return (pl.program_id(0) * pl.num_programs(1) + pl.program_id(1),
            pl.num_programs(0) * pl.num_programs(1))


def _ada_kernel(c_ref, w_ref, b_ref, o_ref):
    c = c_ref[...]
    c_act = (c * jax.nn.sigmoid(c)).astype(BF16)
    o_ref[0] = jnp.dot(c_act, w_ref[0].astype(BF16), preferred_element_type=F32) + b_ref[0]


def _ada(c, w_ada, b_ada):
    depth = w_ada.shape[0]
    batch = c.shape[0]
    n_col = w_ada.shape[2] // D_MODEL
    return pl.pallas_call(
        _ada_kernel,
        out_shape=jax.ShapeDtypeStruct((depth, batch, n_col * D_MODEL), F32),
        grid=(depth, n_col),
        in_specs=[
            pl.BlockSpec((batch, D_MODEL), lambda l, j: (0, 0)),
            pl.BlockSpec((1, D_MODEL, D_MODEL), lambda l, j: (l, 0, j)),
            pl.BlockSpec((1, 1, D_MODEL), lambda l, j: (l, 0, j)),
        ],
        out_specs=pl.BlockSpec((1, batch, D_MODEL), lambda l, j: (l, 0, j)),
        compiler_params=pltpu.CompilerParams(
            dimension_semantics=("arbitrary", "arbitrary"), vmem_limit_bytes=VMEM_LIMIT),
        name="ada",
    )(c, w_ada, b_ada.reshape(depth, 1, -1))


def _route(logits_t, bias):
    s = jax.nn.sigmoid(logits_t)
    sel = s + bias
    sel_r = [sel[e:e + 1] for e in range(N_EXPERTS)]
    s_r = [s[e:e + 1] for e in range(N_EXPERTS)]

    def group_score(g):
        r = sel_r[GROUP_SIZE * g:GROUP_SIZE * (g + 1)]
        best = r[PAIR_A[0]] + r[PAIR_B[0]]
        for p in range(1, N_PAIRS):
            best = jnp.maximum(best, r[PAIR_A[p]] + r[PAIR_B[p]])
        return best

    best_v = group_score(0)
    best_g = jnp.zeros_like(best_v)
    for g in range(1, N_GROUPS):
        gs = group_score(g)
        upd = gs > best_v
        best_v = jnp.where(upd, gs, best_v)
        best_g = jnp.where(upd, float(g), best_g)

    def pick(rows, i):
        out = rows[i]
        for g in range(1, N_GROUPS):
            out = jnp.where(best_g == float(g), rows[GROUP_SIZE * g + i], out)
        return out

    selg = [pick(sel_r, i) for i in range(GROUP_SIZE)]
    sg = [pick(s_r, i) for i in range(GROUP_SIZE)]
    chosen = []
    for i in range(GROUP_SIZE):
        beaten = jnp.zeros_like(best_v)
        for j in range(GROUP_SIZE):
            if j == i:
                continue
            wins = selg[j] > selg[i]
            if j < i:
                wins = wins | (selg[j] == selg[i])
            beaten = beaten + wins.astype(F32)
        chosen.append(beaten < 2.0)
    m0, m1, m2, m3 = chosen
    pair = jnp.where(m0, jnp.where(m1, 0.0, jnp.where(m2, 1.0, 2.0)),
                     jnp.where(m1, jnp.where(m2, 3.0, 4.0), 5.0))
    s_a = jnp.where(m0, sg[0], jnp.where(m1, sg[1], sg[2]))
    s_b = jnp.where(m3, sg[3], jnp.where(m2, sg[2], sg[1]))
    tot = s_a + s_b
    cls = best_g * float(N_PAIRS) + pair
    zero = jnp.zeros_like(cls)
    return jnp.concatenate([cls, s_a / tot, s_b / tot] + [zero] * (SUBLANES - 3), axis=0)


def _mix_kernel(*refs, prev_moe):
    if prev_moe:
        pos_ref, x_ref, y_hbm, pmod_ref, *refs = refs
        *refs, ybuf, ysem = refs
    else:
        x_ref, *refs = refs
    (mod_ref, n1_ref, n2_ref, win_ref, cw_ref, pw_ref, ps_ref, wout_ref, wr_ref, rb_ref,
     x1_ref, h_ref, ri_ref, gbuf, vbuf) = refs
    ts = x_ref.shape[1]
    st = pl.program_id(1)
    mod = mod_ref[0]
    sh1, sc1, g1, sh2, sc2 = (mod[i:i + 1] for i in range(5))
    cw = cw_ref[...]
    ps = ps_ref[...]
    x = x_ref[0]

    if prev_moe:
        step, n_steps = _linear_step()
        prev_rows = _RowGather(pos_ref, y_hbm, ybuf, ysem, ts)

        @pl.when(step == 0)
        def _():
            prev_rows.zero_spare()
            prev_rows.start(0)

        prev_rows.wait(step)
        x = x + pmod_ref[0][5:6] * _rows_to_tokens(prev_rows.rows(step), ts)
        cw = cw + prev_rows.start_anchored(step + 1, 0)

    @pl.when(st == 0)
    def _():
        gbuf[0:CONV_HALO] = jnp.zeros((CONV_HALO, CONV_WIDTH), F32)
        vbuf[0:POOL_HALO] = jnp.zeros((POOL_HALO, POOL_WIDTH), F32)

    h = _norm_mod(x, n1_ref[...], sh1, sc1).astype(BF16)
    proj = jnp.dot(h, win_ref[...], preferred_element_type=F32)
    u_b = proj[:, 0:CONV_WIDTH]
    u_c = proj[:, CONV_WIDTH:2 * CONV_WIDTH]
    u_h = proj[:, 2 * CONV_WIDTH:3 * CONV_WIDTH]
    v = proj[:, 3 * CONV_WIDTH:]
    if prev_moe:
        ps = ps + prev_rows.start_anchored(step + 1, 1)

    g = u_c * u_h
    gbuf[CONV_HALO:CONV_HALO + ts] = g
    vbuf[POOL_HALO:POOL_HALO + ts] = v

    conv = (cw[0:1] * gbuf[CONV_HALO - 2:CONV_HALO - 2 + ts]
            + cw[1:2] * gbuf[CONV_HALO - 1:CONV_HALO - 1 + ts]
            + cw[2:3] * g)
    y_conv = u_b * conv

    t_pos = (st * ts + lax.broadcasted_iota(jnp.int32, (ts, 1), 0) + 1).astype(F32)
    pooled = []
    for gi, win in enumerate(POOL_WINDOWS):
        lo = gi * POOL_GROUP
        v_g = v[:, lo:lo + POOL_GROUP]
        acc = vbuf[0:POOL_HALO + ts, lo:lo + POOL_GROUP]
        k = 1
        while k < win:
            acc = acc[k:] + acc[:-k]
            k *= 2
        first = POOL_HALO - win + 1
        inv_cnt = 1.0 / jnp.minimum(t_pos, float(win))
        pooled.append(acc[first:first + ts] * inv_cnt - v_g)
    pooled = jnp.concatenate(pooled, axis=1).astype(BF16)
    half = POOL_WIDTH // 2
    y_pool = jnp.concatenate(
        [jnp.dot(pooled[:, 0:half], pw_ref[0], preferred_element_type=F32),
         jnp.dot(pooled[:, half:], pw_ref[1], preferred_element_type=F32)], axis=1) * ps
    if prev_moe:
        g1 = g1 + prev_rows.start_anchored(step + 1, 2)

    gbuf[0:CONV_HALO] = gbuf[ts:ts + CONV_HALO]
    vbuf[0:POOL_HALO] = vbuf[ts:ts + POOL_HALO]

    y_mix = jnp.concatenate([y_conv, y_pool], axis=1).astype(BF16)
    x1 = x + g1 * jnp.dot(y_mix, wout_ref[...], preferred_element_type=F32)
    x1_ref[0] = x1
    if prev_moe:
        sh2 = sh2 + prev_rows.start_anchored(step + 1, 3)

    h2 = _norm_mod(x1, n2_ref[...], sh2, sc2)
    logits_t = lax.dot_general(wr_ref[...], h2.astype(BF16), (((1,), (1,)), ((), ())),
                               preferred_element_type=F32)
    info = _route(logits_t, rb_ref[...])
    ri_ref[...] = info

    packed = pltpu.pack_elementwise([h2[:, :D_MODEL // 2], h2[:, D_MODEL // 2:]], packed_dtype=BF16)
    for j in range(PACKED_SLABS):
        h_ref[pl.ds(j, ts, stride=SUBLANES), :] = packed[:, j * LANES:(j + 1) * LANES]
    info_t = jnp.concatenate([info, jnp.zeros((LANES - SUBLANES, ts), F32)], axis=0).T
    h_ref[pl.ds(INFO_SLAB, ts, stride=SUBLANES), :] = lax.bitcast_convert_type(info_t, U32)
    for j in range(INFO_SLAB + 1, SUBLANES):
        h_ref[pl.ds(j, ts, stride=SUBLANES), :] = jnp.zeros((ts, LANES), U32)

    if prev_moe:
        @pl.when(step == n_steps - 1)
        def _():
            prev_rows.wait(step + 1)


def _mix(x, prev, mod, n1, n2, win, cw, pw, ps, wout, wr, rb):
    batch, seq, _ = x.shape
    ts = MIX_TOKENS
    n_s = seq // ts
    tokens = batch * seq
    const = lambda shape: pl.BlockSpec(shape, lambda b, s, *_: (0,) * len(shape))
    prev_specs = [] if prev is None else [
        pl.BlockSpec(memory_space=pl.ANY),
        pl.BlockSpec((1, 6, D_MODEL), lambda b, s, *_: (b, 0, 0))]
    return pl.pallas_call(
        functools.partial(_mix_kernel, prev_moe=prev is not None),
        out_shape=(
            jax.ShapeDtypeStruct((batch, seq, D_MODEL), F32),
            jax.ShapeDtypeStruct((tokens * SUBLANES, LANES), U32),
            jax.ShapeDtypeStruct((SUBLANES, tokens), F32),
        ),
        grid_spec=pltpu.PrefetchScalarGridSpec(
            num_scalar_prefetch=0 if prev is None else 1,
            grid=(batch, n_s),
            in_specs=[
                pl.BlockSpec((1, ts, D_MODEL), lambda b, s, *_: (b, s, 0)),
                *prev_specs,
                pl.BlockSpec((1, 6, D_MODEL), lambda b, s, *_: (b, 0, 0)),
                const((1, D_MODEL)), const((1, D_MODEL)),
                const((D_MODEL, IN_COLS)),
                const((3, CONV_WIDTH)),
                const((2, POOL_WIDTH // 2, POOL_WIDTH // 2)),
                const((1, POOL_WIDTH)),
                const((D_MODEL, D_MODEL)),
                const((N_EXPERTS, D_MODEL)),
                const((N_EXPERTS, 1)),
            ],
            out_specs=(
                pl.BlockSpec((1, ts, D_MODEL), lambda b, s, *_: (b, s, 0)),
                pl.BlockSpec((ts * SUBLANES, LANES), lambda b, s, *_: (b * n_s + s, 0)),
                pl.BlockSpec((SUBLANES, ts), lambda b, s, *_: (0, b * n_s + s)),
            ),
            scratch_shapes=[
                pltpu.VMEM((ts + CONV_HALO, CONV_WIDTH), F32),
                pltpu.VMEM((ts + POOL_HALO, POOL_WIDTH), F32),
                *([] if prev is None else _RowGather.scratch(ts)),
            ],
        ),
        compiler_params=pltpu.CompilerParams(
            dimension_semantics=("arbitrary", "arbitrary"), vmem_limit_bytes=VMEM_LIMIT),
        name="mix",
    )(*(() if prev is None else prev[:1]), x, *(() if prev is None else prev[1:]),
      mod, n1, n2, win, cw, pw, ps, wout, wr, rb)


def _dispatch_kernel(pos_ref, h_ref, z_ref, xs_hbm, sem, *, token_steps):
    n = DISPATCH_TOKENS
    i = pl.program_id(0)
    base = i * n

    def scatter(src_ref):
        def issue(q, carry):
            for k in range(DMA_QUEUES):
                r = q * DMA_QUEUES + k
                _row_copy(src_ref, r, xs_hbm, pos_ref[base + r], sem).start(priority=k)
            return carry

        lax.fori_loop(0, n // DMA_QUEUES, issue, 0, unroll=4)
        pltpu.make_async_copy(src_ref, xs_hbm.at[pl.ds(0, n * SUBLANES)], sem).wait()

    @pl.when(i < token_steps)
    def _():
        scatter(h_ref)

    @pl.when(i >= token_steps)
    def _():
        scatter(z_ref)


def _dispatch(pos_all, h_rows, n_rows):
    n = DISPATCH_TOKENS
    token_steps = h_rows.shape[0] // SUBLANES // n
    zero_rows = jnp.zeros((n * SUBLANES, LANES), U32)
    return pl.pallas_call(
        functools.partial(_dispatch_kernel, token_steps=token_steps),
        out_shape=jax.ShapeDtypeStruct((n_rows * SUBLANES, LANES), U32),
        grid_spec=pltpu.PrefetchScalarGridSpec(
            num_scalar_prefetch=1,
            grid=(pos_all.shape[0] // n,),
            in_specs=[pl.BlockSpec((n * SUBLANES, LANES),
                                   lambda i, p: (jnp.minimum(i, token_steps - 1), 0)),
                      pl.BlockSpec((n * SUBLANES, LANES), lambda i, p: (0, 0))],
            out_specs=pl.BlockSpec(memory_space=pl.ANY),
            scratch_shapes=[pltpu.SemaphoreType.DMA],
        ),
        compiler_params=pltpu.CompilerParams(
            dimension_semantics=("arbitrary",), vmem_limit_bytes=VMEM_LIMIT),
        name="dispatch",
    )(pos_all, h_rows, zero_rows)


def _ffn_kernel(ea_ref, eb_ref, valid_ref, xs_ref, wga_ref, wgb_ref, wda_ref, wdb_ref, y_ref):
    del ea_ref, eb_ref
    tm = FFN_ROWS
    i = pl.program_id(0)

    @pl.when(valid_ref[i] != 0)
    def _():
        slabs = [xs_ref[pl.ds(j, tm, stride=SUBLANES), :] for j in range(INFO_SLAB + 1)]
        packed = jnp.concatenate(slabs[:PACKED_SLABS], axis=1)
        xg = jnp.concatenate(
            [pltpu.unpack_elementwise(packed, index=k, packed_dtype=BF16, unpacked_dtype=F32)
             for k in range(2)], axis=1).astype(BF16)
        info = lax.bitcast_convert_type(slabs[INFO_SLAB], F32)

        def act(wg_ref, w):
            gu = jnp.dot(xg, wg_ref[0], preferred_element_type=F32)
            gate = gu[:, :D_EXPERT]
            up = gu[:, D_EXPERT:]
            return (gate * jax.nn.sigmoid(gate) * up * w).astype(BF16)

        y = (jnp.dot(act(wga_ref, info[:, 1:2]), wda_ref[0], preferred_element_type=F32)
             + jnp.dot(act(wgb_ref, info[:, 2:3]), wdb_ref[0], preferred_element_type=F32))
        for j in range(SUBLANES):
            y_ref[pl.ds(j, tm, stride=SUBLANES), :] = y[:, j * LANES:(j + 1) * LANES]

    @pl.when(valid_ref[i] == 0)
    def _():
        y_ref[...] = jnp.zeros(y_ref.shape, F32)


def _ffn(tile_ea, tile_eb, tile_valid, xs_rows, wgu, wd):
    n_tiles = tile_ea.shape[0]
    tm = FFN_ROWS
    return pl.pallas_call(
        _ffn_kernel,
        out_shape=jax.ShapeDtypeStruct((n_tiles * tm * SUBLANES, LANES), F32),
        grid_spec=pltpu.PrefetchScalarGridSpec(
            num_scalar_prefetch=3,
            grid=(n_tiles,),
            in_specs=[
                pl.BlockSpec((tm * SUBLANES, LANES), lambda i, ea, eb, va: (i, 0)),
                pl.BlockSpec((1, D_MODEL, 2 * D_EXPERT), lambda i, ea, eb, va: (ea[i], 0, 0)),
                pl.BlockSpec((1, D_MODEL, 2 * D_EXPERT), lambda i, ea, eb, va: (eb[i], 0, 0)),
                pl.BlockSpec((1, D_EXPERT, D_MODEL), lambda i, ea, eb, va: (ea[i], 0, 0)),
                pl.BlockSpec((1, D_EXPERT, D_MODEL), lambda i, ea, eb, va: (eb[i], 0, 0)),
            ],
            out_specs=pl.BlockSpec((tm * SUBLANES, LANES), lambda i, ea, eb, va: (i, 0)),
        ),
        compiler_params=pltpu.CompilerParams(
            dimension_semantics=("arbitrary",), vmem_limit_bytes=VMEM_LIMIT),
        name="ffn",
    )(tile_ea, tile_eb, tile_valid, xs_rows, wgu, wgu, wd, wd)


def _fin_kernel(pos_ref, x1_ref, mod_ref, fg_ref, y_hbm, o_ref, ybuf, ysem):
    tf = x1_ref.shape[1]
    step, n_steps = _linear_step()
    last_rows = _RowGather(pos_ref, y_hbm, ybuf, ysem, tf)

    @pl.when(step == 0)
    def _():
        last_rows.zero_spare()
        last_rows.start(0)

    last_rows.start(step + 1)
    last_rows.wait(step)
    x2 = x1_ref[0] + mod_ref[0][5:6] * _rows_to_tokens(last_rows.rows(step), tf)
    ms = jnp.mean(x2 * x2, axis=-1, keepdims=True)
    o_ref[0] = x2 * lax.rsqrt(ms + EPS) * fg_ref[...]

    @pl.when(step == n_steps - 1)
    def _():
        last_rows.wait(step + 1)


def _fin(pos, x1, y_rows, mod, final_g):
    batch, seq, _ = x1.shape
    tf = FIN_TOKENS
    return pl.pallas_call(
        _fin_kernel,
        out_shape=jax.ShapeDtypeStruct((batch, seq, D_MODEL), F32),
        grid_spec=pltpu.PrefetchScalarGridSpec(
            num_scalar_prefetch=1,
            grid=(batch, seq // tf),
            in_specs=[
                pl.BlockSpec((1, tf, D_MODEL), lambda b, s, p: (b, s, 0)),
                pl.BlockSpec((1, 6, D_MODEL), lambda b, s, p: (b, 0, 0)),
                pl.BlockSpec((1, D_MODEL), lambda b, s, p: (0, 0)),
                pl.BlockSpec(memory_space=pl.ANY),
            ],
            out_specs=pl.BlockSpec((1, tf, D_MODEL), lambda b, s, p: (b, s, 0)),
            scratch_shapes=_RowGather.scratch(tf),
        ),
        compiler_params=pltpu.CompilerParams(
            dimension_semantics=("arbitrary", "arbitrary"), vmem_limit_bytes=VMEM_LIMIT),
        name="fin",
    )(pos, x1, mod, final_g, y_rows)


def _plan(cls, n_tiles):
    tm = FFN_ROWS
    onehot = (cls[:, None] == jnp.arange(N_CLASSES, dtype=jnp.int32)[None, :]).astype(jnp.int32)
    csum = jnp.cumsum(onehot, axis=0)
    rank = jnp.take_along_axis(csum, cls[:, None], axis=1)[:, 0] - 1
    counts = csum[-1]
    tiles = (counts + tm - 1) // tm
    tile_end = jnp.cumsum(tiles)
    tile_start = tile_end - tiles
    pos = (tile_start[cls] * tm + rank).astype(jnp.int32)
    j = jnp.arange(n_tiles, dtype=jnp.int32)
    total = tile_end[-1]
    j_eff = jnp.minimum(j, total - 1)
    tile_cls = jnp.sum((tile_end[None, :] <= j_eff[:, None]).astype(jnp.int32), axis=1)
    group = tile_cls // N_PAIRS
    pair = tile_cls % N_PAIRS
    ea = group * GROUP_SIZE + jnp.asarray(PAIR_A, jnp.int32)[pair]
    eb = group * GROUP_SIZE + jnp.asarray(PAIR_B, jnp.int32)[pair]
    slot = jnp.arange(tm, dtype=jnp.int32)[None, :]
    is_pad = slot < (tiles * tm - counts)[:, None]
    spare_rank = (jnp.cumsum((~is_pad).reshape(-1).astype(jnp.int32)) - 1).reshape(N_CLASSES, tm)
    pad_pos = jnp.where(is_pad, (tile_start * tm + counts)[:, None] + slot, total * tm + spare_rank)
    pos_all = jnp.concatenate([pos, pad_pos.reshape(-1).astype(jnp.int32)])
    assert pos_all.shape[0] % DISPATCH_TOKENS == 0
    assert MIX_TOKENS == FIN_TOKENS
    pos_ahead = jnp.concatenate([pos, jnp.zeros((MIX_TOKENS,), jnp.int32)])
    return pos_all, pos_ahead, ea, eb, (j < total).astype(jnp.int32)


def kernel(x, c, w_in, conv_w, pool_w, pool_scale, w_out, norm1_g, norm2_g, w_ada, b_ada,
           w_router, router_bias, w_gate_up, w_down, final_g):
    batch, seq, _ = x.shape
    depth = w_in.shape[0]
    tokens = batch * seq
    n_tiles = tokens // FFN_ROWS + N_CLASSES

    mod = _ada(c, w_ada, b_ada).reshape(depth, batch, 6, D_MODEL)
    wr_t = w_router.T.astype(BF16)
    rb = router_bias.reshape(N_EXPERTS, 1).astype(F32)
    fg = final_g.reshape(1, D_MODEL)
    z = jnp.zeros((depth, POOL_GROUP, POOL_GROUP), F32)
    pw_blocks = jnp.stack([
        jnp.concatenate([jnp.concatenate([pool_w[:, 2 * k], z], axis=2),
                         jnp.concatenate([z, pool_w[:, 2 * k + 1]], axis=2)], axis=1)
        for k in range(2)], axis=1).astype(BF16)

    prev = None
    for l in range(depth):
        x, h_rows, info = _mix(
            x, prev, mod[l], norm1_g[l].reshape(1, -1), norm2_g[l].reshape(1, -1),
            w_in[l].astype(BF16), conv_w[l], pw_blocks[l], pool_scale[l].reshape(1, -1),
            w_out[l].astype(BF16), wr_t, rb)
        pos_all, pos_ahead, ea, eb, valid = _plan(info[0].astype(jnp.int32), n_tiles)
        xs_rows = _dispatch(pos_all, h_rows, n_tiles * FFN_ROWS)
        y_rows = _ffn(ea, eb, valid, xs_rows, w_gate_up[l].astype(BF16), w_down[l].astype(BF16))
        prev = (pos_ahead, y_rows, mod[l])
    return _fin(pos_ahead, x, y_rows, mod[depth - 1], fg)
```

```python
import functools

import jax
import jax.numpy as jnp
from jax import lax
from jax.experimental import pallas as pl
from jax.experimental.pallas import tpu as pltpu

D_MODEL = 1024
CONV_WIDTH = 512
POOL_WIDTH = 512
IN_COLS = 3 * CONV_WIDTH + POOL_WIDTH
POOL_WINDOWS = (2, 4, 8, 16)
POOL_GROUP = 128
N_EXPERTS = 16
N_GROUPS = 4
GROUP_SIZE = 4
D_EXPERT = 512
EPS = 1e-6

LANES = 128
SUBLANES = 8
PACKED_SLABS = D_MODEL // 2 // LANES
INFO_SLAB = PACKED_SLABS
CONV_HALO = SUBLANES
POOL_HALO = 2 * SUBLANES

PAIR_A = (0, 0, 0, 1, 1, 2)
PAIR_B = (1, 2, 3, 2, 3, 3)
N_PAIRS = len(PAIR_A)
N_CLASSES = N_GROUPS * N_PAIRS

MIX_TOKENS = 512
FFN_ROWS = 256
FIN_TOKENS = 512
DISPATCH_TOKENS = 1024
TOKEN_SLOTS = 2
DMA_GROUPS = 4
DMA_QUEUES = 2
VMEM_LIMIT = 56 * 1024 * 1024

F32 = jnp.float32
BF16 = jnp.bfloat16
U32 = jnp.uint32


def _norm_mod(v, gain, shift, scale):
    ms = jnp.mean(v * v, axis=-1, keepdims=True)
    return (v * lax.rsqrt(ms + EPS)) * (gain * (1.0 + scale)) + shift


def _rows_to_tokens(rows_ref, n):
    return jnp.concatenate(
        [rows_ref[pl.ds(j, n, stride=SUBLANES), :] for j in range(SUBLANES)], axis=1)


def _row_copy(src, src_row, dst, dst_row, sem):
    return pltpu.make_async_copy(
        src.at[pl.ds(pl.multiple_of(src_row * SUBLANES, SUBLANES), SUBLANES)],
        dst.at[pl.ds(pl.multiple_of(dst_row * SUBLANES, SUBLANES), SUBLANES)], sem)


class _RowGather:
    def __init__(self, idx_ref, src_hbm, buf, sems, n):
        self.idx_ref, self.src, self.buf, self.sems, self.n = idx_ref, src_hbm, buf, sems, n
        self.slots = buf.shape[0]

    def zero_spare(self):
        spare = (self.slots, SUBLANES, LANES)
        self.buf[:, self.n * SUBLANES:, :] = jnp.zeros(spare, self.buf.dtype)

    def start(self, tile, part=0, parts=1):
        slot = lax.rem(tile, self.slots)
        for r in range(part * self.n // parts, (part + 1) * self.n // parts):
            _row_copy(self.src, self.idx_ref[tile * self.n + r], self.buf.at[slot], r,
                      self.sems.at[slot]).start(priority=r % DMA_QUEUES)

    def start_anchored(self, tile, part):
        self.start(tile, part, DMA_GROUPS)
        spare = self.buf[lax.rem(tile + self.slots, self.slots), self.n * SUBLANES:, :]
        bits = lax.bitcast_convert_type(spare, U32)
        return lax.bitcast_convert_type(((bits >> 16) >> 16)[0:1, 0:1], F32)

    def wait(self, tile):
        slot = lax.rem(tile, self.slots)
        rows = pl.ds(0, self.n * SUBLANES)
        pltpu.make_async_copy(self.src.at[rows], self.buf.at[slot, rows], self.sems.at[slot]).wait()

    def rows(self, tile):
        return self.buf.at[lax.rem(tile, self.slots)]

    @staticmethod
    def scratch(n):
        return [pltpu.VMEM((TOKEN_SLOTS, (n + 1) * SUBLANES, LANES), F32),
                pltpu.SemaphoreType.DMA((TOKEN_SLOTS,))]


def _linear_step():
    return (pl.program_id(0) * pl.num_programs(1) + pl.program_id(1),
            pl.num_programs(0) * pl.num_programs(1))


def _ada_kernel(c_ref, w_ref, b_ref, o_ref):
    c = c_ref[...]
    c_act = (c * jax.nn.sigmoid(c)).astype(BF16)
    o_ref[0] = jnp.dot(c_act, w_ref[0].astype(BF16), preferred_element_type=F32) + b_ref[0]


def _ada(c, w_ada, b_ada):
    depth = w_ada.shape[0]
    batch = c.shape[0]
    n_col = w_ada.shape[2] // D_MODEL
    return pl.pallas_call(
        _ada_kernel,
        out_shape=jax.ShapeDtypeStruct((depth, batch, n_col * D_MODEL), F32),
        grid=(depth, n_col),
        in_specs=[
            pl.BlockSpec((batch, D_MODEL), lambda l, j: (0, 0)),
            pl.BlockSpec((1, D_MODEL, D_MODEL), lambda l, j: (l, 0, j)),
            pl.BlockSpec((1, 1, D_MODEL), lambda l, j: (l, 0, j)),
        ],
        out_specs=pl.BlockSpec((1, batch, D_MODEL), lambda l, j: (l, 0, j)),
        compiler_params=pltpu.CompilerParams(
            dimension_semantics=("arbitrary", "arbitrary"), vmem_limit_bytes=VMEM_LIMIT),
        name="ada",
    )(c, w_ada, b_ada.reshape(depth, 1, -1))


def _route(logits_t, bias):
    s = jax.nn.sigmoid(logits_t)
    sel = s + bias
    sel_r = [sel[e:e + 1] for e in range(N_EXPERTS)]
    s_r = [s[e:e + 1] for e in range(N_EXPERTS)]

    def group_score(g):
        r = sel_r[GROUP_SIZE * g:GROUP_SIZE * (g + 1)]
        best = r[PAIR_A[0]] + r[PAIR_B[0]]
        for p in range(1, N_PAIRS):
            best = jnp.maximum(best, r[PAIR_A[p]] + r[PAIR_B[p]])
        return best

    best_v = group_score(0)
    best_g = jnp.zeros_like(best_v)
    for g in range(1, N_GROUPS):
        gs = group_score(g)
        upd = gs > best_v
        best_v = jnp.where(upd, gs, best_v)
        best_g = jnp.where(upd, float(g), best_g)

    def pick(rows, i):
        out = rows[i]
        for g in range(1, N_GROUPS):
            out = jnp.where(best_g == float(g), rows[GROUP_SIZE * g + i], out)
        return out

    selg = [pick(sel_r, i) for i in range(GROUP_SIZE)]
    sg = [pick(s_r, i) for i in range(GROUP_SIZE)]
    chosen = []
    for i in range(GROUP_SIZE):
        beaten = jnp.zeros_like(best_v)
        for j in range(GROUP_SIZE):
            if j == i:
                continue
            wins = selg[j] > selg[i]
            if j < i:
                wins = wins | (selg[j] == selg[i])
            beaten = beaten + wins.astype(F32)
        chosen.append(beaten < 2.0)
    m0, m1, m2, m3 = chosen
    pair = jnp.where(m0, jnp.where(m1, 0.0, jnp.where(m2, 1.0, 2.0)),
                     jnp.where(m1, jnp.where(m2, 3.0, 4.0), 5.0))
    s_a = jnp.where(m0, sg[0], jnp.where(m1, sg[1], sg[2]))
    s_b = jnp.where(m3, sg[3], jnp.where(m2, sg[2], sg[1]))
    tot = s_a + s_b
    cls = best_g * float(N_PAIRS) + pair
    zero = jnp.zeros_like(cls)
    return jnp.concatenate([cls, s_a / tot, s_b / tot] + [zero] * (SUBLANES - 3), axis=0)


def _mix_kernel(*refs, prev_moe):
    if prev_moe:
        pos_ref, x_ref, y_hbm, pmod_ref, *refs = refs
        *refs, ybuf, ysem = refs
    else:
        x_ref, *refs = refs
    (mod_ref, n1_ref, n2_ref, win_ref, cw_ref, pw_ref, ps_ref, wout_ref, wr_ref, rb_ref,
     x1_ref, h_ref, ri_ref, gbuf, vbuf) = refs
    ts = x_ref.shape[1]
    st = pl.program_id(1)
    mod = mod_ref[0]
    sh1, sc1, g1, sh2, sc2 = (mod[i:i + 1] for i in range(5))
    cw = cw_ref[...]
    ps = ps_ref[...]
    x = x_ref[0]

    if prev_moe:
        step, n_steps = _linear_step()
        prev_rows = _RowGather(pos_ref, y_hbm, ybuf, ysem, ts)

        @pl.when(step == 0)
        def _():
            prev_rows.zero_spare()
            prev_rows.start(0)

        prev_rows.wait(step)
        x = x + pmod_ref[0][5:6] * _rows_to_tokens(prev_rows.rows(step), ts)
        cw = cw + prev_rows.start_anchored(step + 1, 0)

    @pl.when(st == 0)
    def _():
        gbuf[0:CONV_HALO] = jnp.zeros((CONV_HALO, CONV_WIDTH), F32)
        vbuf[0:POOL_HALO] = jnp.zeros((POOL_HALO, POOL_WIDTH), F32)

    h = _norm_mod(x, n1_ref[...], sh1, sc1).astype(BF16)
    proj = jnp.dot(h, win_ref[0], preferred_element_type=F32)
    u_b = proj[:, 0:CONV_WIDTH]
    u_c = proj[:, CONV_WIDTH:2 * CONV_WIDTH]
    u_h = proj[:, 2 * CONV_WIDTH:3 * CONV_WIDTH]
    v = proj[:, 3 * CONV_WIDTH:]
    if prev_moe:
        ps = ps + prev_rows.start_anchored(step + 1, 1)

    g = u_c * u_h
    gbuf[CONV_HALO:CONV_HALO + ts] = g
    vbuf[POOL_HALO:POOL_HALO + ts] = v

    conv = (cw[0:1] * gbuf[CONV_HALO - 2:CONV_HALO - 2 + ts]
            + cw[1:2] * gbuf[CONV_HALO - 1:CONV_HALO - 1 + ts]
            + cw[2:3] * g)
    y_conv = u_b * conv

    t_pos = (st * ts + lax.broadcasted_iota(jnp.int32, (ts, 1), 0) + 1).astype(F32)
    pooled = []
    for gi, win in enumerate(POOL_WINDOWS):
        lo = gi * POOL_GROUP
        v_g = v[:, lo:lo + POOL_GROUP]
        acc = vbuf[0:POOL_HALO + ts, lo:lo + POOL_GROUP]
        k = 1
        while k < win:
            acc = acc[k:] + acc[:-k]
            k *= 2
        first = POOL_HALO - win + 1
        inv_cnt = 1.0 / jnp.minimum(t_pos, float(win))
        pooled.append(acc[first:first + ts] * inv_cnt - v_g)
    pooled = jnp.concatenate(pooled, axis=1).astype(BF16)
    half = POOL_WIDTH // 2
    y_pool = jnp.concatenate(
        [jnp.dot(pooled[:, 0:half], pw_ref[0], preferred_element_type=F32),
         jnp.dot(pooled[:, half:], pw_ref[1], preferred_element_type=F32)], axis=1) * ps
    if prev_moe:
        g1 = g1 + prev_rows.start_anchored(step + 1, 2)

    gbuf[0:CONV_HALO] = gbuf[ts:ts + CONV_HALO]
    vbuf[0:POOL_HALO] = vbuf[ts:ts + POOL_HALO]

    y_mix = jnp.concatenate([y_conv, y_pool], axis=1).astype(BF16)
    x1 = x + g1 * jnp.dot(y_mix, wout_ref[0], preferred_element_type=F32)
    x1_ref[0] = x1
    if prev_moe:
        sh2 = sh2 + prev_rows.start_anchored(step + 1, 3)

    h2 = _norm_mod(x1, n2_ref[...], sh2, sc2)
    logits_t = lax.dot_general(wr_ref[...], h2.astype(BF16), (((1,), (1,)), ((), ())),
                               preferred_element_type=F32)
    info = _route(logits_t, rb_ref[...])
    ri_ref[...] = info

    packed = pltpu.pack_elementwise([h2[:, :D_MODEL // 2], h2[:, D_MODEL // 2:]], packed_dtype=BF16)
    for j in range(PACKED_SLABS):
        h_ref[pl.ds(j, ts, stride=SUBLANES), :] = packed[:, j * LANES:(j + 1) * LANES]
    info_t = jnp.concatenate([info, jnp.zeros((LANES - SUBLANES, ts), F32)], axis=0).T
    h_ref[pl.ds(INFO_SLAB, ts, stride=SUBLANES), :] = lax.bitcast_convert_type(info_t, U32)
    for j in range(INFO_SLAB + 1, SUBLANES):
        h_ref[pl.ds(j, ts, stride=SUBLANES), :] = jnp.zeros((ts, LANES), U32)

    if prev_moe:
        @pl.when(step == n_steps - 1)
        def _():
            prev_rows.wait(step + 1)


def _mix(layer, x, prev, mod, n1, n2, win, cw, pw, ps, wout, wr, rb):
    batch, seq, _ = x.shape
    of_layer = lambda shape: pl.BlockSpec((1,) + shape, lambda b, s, *_: (layer, 0, 0))
    ts = MIX_TOKENS
    n_s = seq // ts
    tokens = batch * seq
    const = lambda shape: pl.BlockSpec(shape, lambda b, s, *_: (0,) * len(shape))
    prev_specs = [] if prev is None else [
        pl.BlockSpec(memory_space=pl.ANY),
        pl.BlockSpec((1, 6, D_MODEL), lambda b, s, *_: (b, 0, 0))]
    return pl.pallas_call(
        functools.partial(_mix_kernel, prev_moe=prev is not None),
        out_shape=(
            jax.ShapeDtypeStruct((batch, seq, D_MODEL), F32),
            jax.ShapeDtypeStruct((tokens * SUBLANES, LANES), U32),
            jax.ShapeDtypeStruct((SUBLANES, tokens), F32),
        ),
        grid_spec=pltpu.PrefetchScalarGridSpec(
            num_scalar_prefetch=0 if prev is None else 1,
            grid=(batch, n_s),
            in_specs=[
                pl.BlockSpec((1, ts, D_MODEL), lambda b, s, *_: (b, s, 0)),
                *prev_specs,
                pl.BlockSpec((1, 6, D_MODEL), lambda b, s, *_: (b, 0, 0)),
                const((1, D_MODEL)), const((1, D_MODEL)),
                of_layer((D_MODEL, IN_COLS)),
                const((3, CONV_WIDTH)),
                const((2, POOL_WIDTH // 2, POOL_WIDTH // 2)),
                const((1, POOL_WIDTH)),
                of_layer((D_MODEL, D_MODEL)),
                const((N_EXPERTS, D_MODEL)),
                const((N_EXPERTS, 1)),
            ],
            out_specs=(
                pl.BlockSpec((1, ts, D_MODEL), lambda b, s, *_: (b, s, 0)),
                pl.BlockSpec((ts * SUBLANES, LANES), lambda b, s, *_: (b * n_s + s, 0)),
                pl.BlockSpec((SUBLANES, ts), lambda b, s, *_: (0, b * n_s + s)),
            ),
            scratch_shapes=[
                pltpu.VMEM((ts + CONV_HALO, CONV_WIDTH), F32),
                pltpu.VMEM((ts + POOL_HALO, POOL_WIDTH), F32),
                *([] if prev is None else _RowGather.scratch(ts)),
            ],
        ),
        compiler_params=pltpu.CompilerParams(
            dimension_semantics=("arbitrary", "arbitrary"), vmem_limit_bytes=VMEM_LIMIT),
        name="mix",
    )(*(() if prev is None else prev[:1]), x, *(() if prev is None else prev[1:]),
      mod, n1, n2, win, cw, pw, ps, wout, wr, rb)


def _dispatch_kernel(pos_ref, h_ref, z_ref, xs_hbm, sem, *, token_steps):
    n = DISPATCH_TOKENS
    i = pl.program_id(0)
    base = i * n

    def scatter(src_ref):
        def issue(q, carry):
            for k in range(DMA_QUEUES):
                r = q * DMA_QUEUES + k
                _row_copy(src_ref, r, xs_hbm, pos_ref[base + r], sem).start(priority=k)
            return carry

        lax.fori_loop(0, n // DMA_QUEUES, issue, 0, unroll=4)
        pltpu.make_async_copy(src_ref, xs_hbm.at[pl.ds(0, n * SUBLANES)], sem).wait()

    @pl.when(i < token_steps)
    def _():
        scatter(h_ref)

    @pl.when(i >= token_steps)
    def _():
        scatter(z_ref)


def _dispatch(pos_all, h_rows, n_rows):
    n = DISPATCH_TOKENS
    token_steps = h_rows.shape[0] // SUBLANES // n
    zero_rows = jnp.zeros((n * SUBLANES, LANES), U32)
    return pl.pallas_call(
        functools.partial(_dispatch_kernel, token_steps=token_steps),
        out_shape=jax.ShapeDtypeStruct((n_rows * SUBLANES, LANES), U32),
        grid_spec=pltpu.PrefetchScalarGridSpec(
            num_scalar_prefetch=1,
            grid=(pos_all.shape[0] // n,),
            in_specs=[pl.BlockSpec((n * SUBLANES, LANES),
                                   lambda i, p: (jnp.minimum(i, token_steps - 1), 0)),
                      pl.BlockSpec((n * SUBLANES, LANES), lambda i, p: (0, 0))],
            out_specs=pl.BlockSpec(memory_space=pl.ANY),
            scratch_shapes=[pltpu.SemaphoreType.DMA],
        ),
        compiler_params=pltpu.CompilerParams(
            dimension_semantics=("arbitrary",), vmem_limit_bytes=VMEM_LIMIT),
        name="dispatch",
    )(pos_all, h_rows, zero_rows)


def _ffn_kernel(ea_ref, eb_ref, valid_ref, xs_ref, wga_ref, wgb_ref, wda_ref, wdb_ref, y_ref):
    del ea_ref, eb_ref
    tm = FFN_ROWS
    i = pl.program_id(0)

    @pl.when(valid_ref[i] != 0)
    def _():
        slabs = [xs_ref[pl.ds(j, tm, stride=SUBLANES), :] for j in range(INFO_SLAB + 1)]
        packed = jnp.concatenate(slabs[:PACKED_SLABS], axis=1)
        xg = jnp.concatenate(
            [pltpu.unpack_elementwise(packed, index=k, packed_dtype=BF16, unpacked_dtype=F32)
             for k in range(2)], axis=1).astype(BF16)
        info = lax.bitcast_convert_type(slabs[INFO_SLAB], F32)

        def act(wg_ref, w):
            gu = jnp.dot(xg, wg_ref[0, 0], preferred_element_type=F32)
            gate = gu[:, :D_EXPERT]
            up = gu[:, D_EXPERT:]
            return (gate * jax.nn.sigmoid(gate) * up * w).astype(BF16)

        y = (jnp.dot(act(wga_ref, info[:, 1:2]), wda_ref[0, 0], preferred_element_type=F32)
             + jnp.dot(act(wgb_ref, info[:, 2:3]), wdb_ref[0, 0], preferred_element_type=F32))
        for j in range(SUBLANES):
            y_ref[pl.ds(j, tm, stride=SUBLANES), :] = y[:, j * LANES:(j + 1) * LANES]

    @pl.when(valid_ref[i] == 0)
    def _():
        y_ref[...] = jnp.zeros(y_ref.shape, F32)


def _ffn(layer, tile_ea, tile_eb, tile_valid, xs_rows, wgu, wd):
    n_tiles = tile_ea.shape[0]
    tm = FFN_ROWS
    return pl.pallas_call(
        _ffn_kernel,
        out_shape=jax.ShapeDtypeStruct((n_tiles * tm * SUBLANES, LANES), F32),
        grid_spec=pltpu.PrefetchScalarGridSpec(
            num_scalar_prefetch=3,
            grid=(n_tiles,),
            in_specs=[
                pl.BlockSpec((tm * SUBLANES, LANES), lambda i, ea, eb, va: (i, 0)),
                pl.BlockSpec((1, 1, D_MODEL, 2 * D_EXPERT),
                             lambda i, ea, eb, va: (layer, ea[i], 0, 0)),
                pl.BlockSpec((1, 1, D_MODEL, 2 * D_EXPERT),
                             lambda i, ea, eb, va: (layer, eb[i], 0, 0)),
                pl.BlockSpec((1, 1, D_EXPERT, D_MODEL), lambda i, ea, eb, va: (layer, ea[i], 0, 0)),
                pl.BlockSpec((1, 1, D_EXPERT, D_MODEL), lambda i, ea, eb, va: (layer, eb[i], 0, 0)),
            ],
            out_specs=pl.BlockSpec((tm * SUBLANES, LANES), lambda i, ea, eb, va: (i, 0)),
        ),
        compiler_params=pltpu.CompilerParams(
            dimension_semantics=("arbitrary",), vmem_limit_bytes=VMEM_LIMIT),
        name="ffn",
    )(tile_ea, tile_eb, tile_valid, xs_rows, wgu, wgu, wd, wd)


def _fin_kernel(pos_ref, x1_ref, mod_ref, fg_ref, y_hbm, o_ref, ybuf, ysem):
    tf = x1_ref.shape[1]
    step, n_steps = _linear_step()
    last_rows = _RowGather(pos_ref, y_hbm, ybuf, ysem, tf)

    @pl.when(step == 0)
    def _():
        last_rows.zero_spare()
        last_rows.start(0)

    last_rows.start(step + 1)
    last_rows.wait(step)
    x2 = x1_ref[0] + mod_ref[0][5:6] * _rows_to_tokens(last_rows.rows(step), tf)
    ms = jnp.mean(x2 * x2, axis=-1, keepdims=True)
    o_ref[0] = x2 * lax.rsqrt(ms + EPS) * fg_ref[...]

    @pl.when(step == n_steps - 1)
    def _():
        last_rows.wait(step + 1)


def _fin(pos, x1, y_rows, mod, final_g):
    batch, seq, _ = x1.shape
    tf = FIN_TOKENS
    return pl.pallas_call(
        _fin_kernel,
        out_shape=jax.ShapeDtypeStruct((batch, seq, D_MODEL), F32),
        grid_spec=pltpu.PrefetchScalarGridSpec(
            num_scalar_prefetch=1,
            grid=(batch, seq // tf),
            in_specs=[
                pl.BlockSpec((1, tf, D_MODEL), lambda b, s, p: (b, s, 0)),
                pl.BlockSpec((1, 6, D_MODEL), lambda b, s, p: (b, 0, 0)),
                pl.BlockSpec((1, D_MODEL), lambda b, s, p: (0, 0)),
                pl.BlockSpec(memory_space=pl.ANY),
            ],
            out_specs=pl.BlockSpec((1, tf, D_MODEL), lambda b, s, p: (b, s, 0)),
            scratch_shapes=_RowGather.scratch(tf),
        ),
        compiler_params=pltpu.CompilerParams(
            dimension_semantics=("arbitrary", "arbitrary"), vmem_limit_bytes=VMEM_LIMIT),
        name="fin",
    )(pos, x1, mod, final_g, y_rows)


def _plan(cls, n_tiles):
    tm = FFN_ROWS
    onehot = (cls[:, None] == jnp.arange(N_CLASSES, dtype=jnp.int32)[None, :]).astype(jnp.int32)
    csum = jnp.cumsum(onehot, axis=0)
    rank = jnp.take_along_axis(csum, cls[:, None], axis=1)[:, 0] - 1
    counts = csum[-1]
    tiles = (counts + tm - 1) // tm
    tile_end = jnp.cumsum(tiles)
    tile_start = tile_end - tiles
    pos = (tile_start[cls] * tm + rank).astype(jnp.int32)
    j = jnp.arange(n_tiles, dtype=jnp.int32)
    total = tile_end[-1]
    j_eff = jnp.minimum(j, total - 1)
    tile_cls = jnp.sum((tile_end[None, :] <= j_eff[:, None]).astype(jnp.int32), axis=1)
    group = tile_cls // N_PAIRS
    pair = tile_cls % N_PAIRS
    ea = group * GROUP_SIZE + jnp.asarray(PAIR_A, jnp.int32)[pair]
    eb = group * GROUP_SIZE + jnp.asarray(PAIR_B, jnp.int32)[pair]
    slot = jnp.arange(tm, dtype=jnp.int32)[None, :]
    is_pad = slot < (tiles * tm - counts)[:, None]
    spare_rank = (jnp.cumsum((~is_pad).reshape(-1).astype(jnp.int32)) - 1).reshape(N_CLASSES, tm)
    pad_pos = jnp.where(is_pad, (tile_start * tm + counts)[:, None] + slot, total * tm + spare_rank)
    pos_all = jnp.concatenate([pos, pad_pos.reshape(-1).astype(jnp.int32)])
    assert pos_all.shape[0] % DISPATCH_TOKENS == 0
    assert MIX_TOKENS == FIN_TOKENS
    pos_ahead = jnp.concatenate([pos, jnp.zeros((MIX_TOKENS,), jnp.int32)])
    return pos_all, pos_ahead, ea, eb, (j < total).astype(jnp.int32)


def kernel(x, c, w_in, conv_w, pool_w, pool_scale, w_out, norm1_g, norm2_g, w_ada, b_ada,
           w_router, router_bias, w_gate_up, w_down, final_g):
    batch, seq, _ = x.shape
    depth = w_in.shape[0]
    tokens = batch * seq
    n_tiles = tokens // FFN_ROWS + N_CLASSES

    mod = _ada(c, w_ada, b_ada).reshape(depth, batch, 6, D_MODEL)
    wr_t = w_router.T.astype(BF16)
    rb = router_bias.reshape(N_EXPERTS, 1).astype(F32)
    fg = final_g.reshape(1, D_MODEL)
    z = jnp.zeros((depth, POOL_GROUP, POOL_GROUP), F32)
    pw_blocks = jnp.stack([
        jnp.concatenate([jnp.concatenate([pool_w[:, 2 * k], z], axis=2),
                         jnp.concatenate([z, pool_w[:, 2 * k + 1]], axis=2)], axis=1)
        for k in range(2)], axis=1).astype(BF16)

    w_in_b, w_out_b = w_in.astype(BF16), w_out.astype(BF16)
    wgu_b, wd_b = w_gate_up.astype(BF16), w_down.astype(BF16)

    prev = None
    for l in range(depth):
        x, h_rows, info = _mix(
            l, x, prev, mod[l], norm1_g[l].reshape(1, -1), norm2_g[l].reshape(1, -1),
            w_in_b, conv_w[l], pw_blocks[l], pool_scale[l].reshape(1, -1), w_out_b, wr_t, rb)
        pos_all, pos_ahead, ea, eb, valid = _plan(info[0].astype(jnp.int32), n_tiles)
        xs_rows = _dispatch(pos_all, h_rows, n_tiles * FFN_ROWS)
        y_rows = _ffn(l, ea, eb, valid, xs_rows, wgu_b, wd_b)
        prev = (pos_ahead, y_rows, mod[l])
    return _fin(pos_ahead, x, y_rows, mod[depth - 1], fg)
```

```python
import functools

import jax
import jax.numpy as jnp
from jax import lax
from jax.experimental import pallas as pl
from jax.experimental.pallas import tpu as pltpu

D_MODEL = 1024
CONV_WIDTH = 512
POOL_WIDTH = 512
IN_COLS = 3 * CONV_WIDTH + POOL_WIDTH
POOL_WINDOWS = (2, 4, 8, 16)
POOL_GROUP = 128
N_EXPERTS = 16
N_GROUPS = 4
GROUP_SIZE = 4
D_EXPERT = 512
EPS = 1e-6

LANES = 128
SUBLANES = 8
PACKED_SLABS = D_MODEL // 2 // LANES
INFO_SLAB = PACKED_SLABS
CONV_HALO = SUBLANES
POOL_HALO = 2 * SUBLANES

PAIR_A = (0, 0, 0, 1, 1, 2)
PAIR_B = (1, 2, 3, 2, 3, 3)
N_PAIRS = len(PAIR_A)
N_CLASSES = N_GROUPS * N_PAIRS

MIX_TOKENS = 512
FFN_ROWS = 256
FIN_TOKENS = 512
DISPATCH_TOKENS = 1024
TOKEN_SLOTS = 2
DMA_GROUPS = 4
DMA_QUEUES = 2
VMEM_LIMIT = 56 * 1024 * 1024

F32 = jnp.float32
BF16 = jnp.bfloat16
U32 = jnp.uint32


def _norm_mod(v, gain, shift, scale):
    ms = jnp.mean(v * v, axis=-1, keepdims=True)
    return (v * lax.rsqrt(ms + EPS)) * (gain * (1.0 + scale)) + shift


def _rows_to_tokens(rows_ref, n):
    return jnp.concatenate(
        [rows_ref[pl.ds(j, n, stride=SUBLANES), :] for j in range(SUBLANES)], axis=1)


def _row_copy(src, src_row, dst, dst_row, sem):
    return pltpu.make_async_copy(
        src.at[pl.ds(pl.multiple_of(src_row * SUBLANES, SUBLANES), SUBLANES)],
        dst.at[pl.ds(pl.multiple_of(dst_row * SUBLANES, SUBLANES), SUBLANES)], sem)


class _RowGather:
    def __init__(self, idx_ref, src_hbm, buf, sems, n):
        self.idx_ref, self.src, self.buf, self.sems, self.n = idx_ref, src_hbm, buf, sems, n
        self.slots = buf.shape[0]

    def zero_spare(self):
        spare = (self.slots, SUBLANES, LANES)
        self.buf[:, self.n * SUBLANES:, :] = jnp.zeros(spare, self.buf.dtype)

    def start(self, tile, part=0, parts=1):
        slot = lax.rem(tile, self.slots)
        for r in range(part * self.n // parts, (part + 1) * self.n // parts):
            _row_copy(self.src, self.idx_ref[tile * self.n + r], self.buf.at[slot], r,
                      self.sems.at[slot]).start(priority=r % DMA_QUEUES)

    def start_anchored(self, tile, part):
        self.start(tile, part, DMA_GROUPS)
        spare = self.buf[lax.rem(tile + self.slots, self.slots), self.n * SUBLANES:, :]
        bits = lax.bitcast_convert_type(spare, U32)
        return lax.bitcast_convert_type(((bits >> 16) >> 16)[0:1, 0:1], F32)

    def wait(self, tile):
        slot = lax.rem(tile, self.slots)
        rows = pl.ds(0, self.n * SUBLANES)
        pltpu.make_async_copy(self.src.at[rows], self.buf.at[slot, rows], self.sems.at[slot]).wait()

    def rows(self, tile):
        return self.buf.at[lax.rem(tile, self.slots)]

    @staticmethod
    def scratch(n):
        return [pltpu.VMEM((TOKEN_SLOTS, (n + 1) * SUBLANES, LANES), F32),
                pltpu.SemaphoreType.DMA((TOKEN_SLOTS,))]


def _linear_step():
    return (pl.program_id(0) * pl.num_programs(1) + pl.program_id(1),
            pl.num_programs(0) * pl.num_programs(1))


def _ada_kernel(c_ref, w_ref, b_ref, o_ref):
    c = c_ref[...]
    c_act = (c * jax.nn.sigmoid(c)).astype(BF16)
    o_ref[0] = jnp.dot(c_act, w_ref[0].astype(BF16), preferred_element_type=F32) + b_ref[0]


def _ada(c, w_ada, b_ada):
    depth = w_ada.shape[0]
    batch = c.shape[0]
    n_col = w_ada.shape[2] // D_MODEL
    return pl.pallas_call(
        _ada_kernel,
        out_shape=jax.ShapeDtypeStruct((depth, batch, n_col * D_MODEL), F32),
        grid=(depth, n_col),
        in_specs=[
            pl.BlockSpec((batch, D_MODEL), lambda l, j: (0, 0)),
            pl.BlockSpec((1, D_MODEL, D_MODEL), lambda l, j: (l, 0, j)),
            pl.BlockSpec((1, 1, D_MODEL), lambda l, j: (l, 0, j)),
        ],
        out_specs=pl.BlockSpec((1, batch, D_MODEL), lambda l, j: (l, 0, j)),
        compiler_params=pltpu.CompilerParams(
            dimension_semantics=("arbitrary", "arbitrary"), vmem_limit_bytes=VMEM_LIMIT),
        name="ada",
    )(c, w_ada, b_ada.reshape(depth, 1, -1))


def _route(logits_t, bias):
    s = jax.nn.sigmoid(logits_t)
    sel = s + bias
    sel_r = [sel[e:e + 1] for e in range(N_EXPERTS)]
    s_r = [s[e:e + 1] for e in range(N_EXPERTS)]

    def group_score(g):
        r = sel_r[GROUP_SIZE * g:GROUP_SIZE * (g + 1)]
        best = r[PAIR_A[0]] + r[PAIR_B[0]]
        for p in range(1, N_PAIRS):
            best = jnp.maximum(best, r[PAIR_A[p]] + r[PAIR_B[p]])
        return best

    best_v = group_score(0)
    best_g = jnp.zeros_like(best_v)
    for g in range(1, N_GROUPS):
        gs = group_score(g)
        upd = gs > best_v
        best_v = jnp.where(upd, gs, best_v)
        best_g = jnp.where(upd, float(g), best_g)

    def pick(rows, i):
        out = rows[i]
        for g in range(1, N_GROUPS):
            out = jnp.where(best_g == float(g), rows[GROUP_SIZE * g + i], out)
        return out

    selg = [pick(sel_r, i) for i in range(GROUP_SIZE)]
    sg = [pick(s_r, i) for i in range(GROUP_SIZE)]
    chosen = []
    for i in range(GROUP_SIZE):
        beaten = jnp.zeros_like(best_v)
        for j in range(GROUP_SIZE):
            if j == i:
                continue
            wins = selg[j] > selg[i]
            if j < i:
                wins = wins | (selg[j] == selg[i])
            beaten = beaten + wins.astype(F32)
        chosen.append(beaten < 2.0)
    m0, m1, m2, m3 = chosen
    pair = jnp.where(m0, jnp.where(m1, 0.0, jnp.where(m2, 1.0, 2.0)),
                     jnp.where(m1, jnp.where(m2, 3.0, 4.0), 5.0))
    s_a = jnp.where(m0, sg[0], jnp.where(m1, sg[1], sg[2]))
    s_b = jnp.where(m3, sg[3], jnp.where(m2, sg[2], sg[1]))
    tot = s_a + s_b
    cls = best_g * float(N_PAIRS) + pair
    zero = jnp.zeros_like(cls)
    return jnp.concatenate([cls, s_a / tot, s_b / tot] + [zero] * (SUBLANES - 3), axis=0)


def _mix_kernel(*refs, prev_moe):
    if prev_moe:
        pos_ref, x_ref, y_hbm, pmod_ref, *refs = refs
        *refs, ybuf, ysem = refs
    else:
        x_ref, *refs = refs
    (mod_ref, n1_ref, n2_ref, win_ref, cw_ref, pw_ref, ps_ref, wout_ref, wr_ref, rb_ref,
     x1_ref, h_ref, ri_ref, gbuf, vbuf) = refs
    ts = x_ref.shape[1]
    st = pl.program_id(1)
    mod = mod_ref[0]
    sh1, sc1, g1, sh2, sc2 = (mod[i:i + 1] for i in range(5))
    cw = cw_ref[...]
    ps = ps_ref[...]
    x = x_ref[0]

    if prev_moe:
        step, n_steps = _linear_step()
        prev_rows = _RowGather(pos_ref, y_hbm, ybuf, ysem, ts)

        @pl.when(step == 0)
        def _():
            prev_rows.zero_spare()
            prev_rows.start(0)

        prev_rows.wait(step)
        x = x + pmod_ref[0][5:6] * _rows_to_tokens(prev_rows.rows(step), ts)
        cw = cw + prev_rows.start_anchored(step + 1, 0)

    @pl.when(st == 0)
    def _():
        gbuf[0:CONV_HALO] = jnp.zeros((CONV_HALO, CONV_WIDTH), F32)
        vbuf[0:POOL_HALO] = jnp.zeros((POOL_HALO, POOL_WIDTH), F32)

    h = _norm_mod(x, n1_ref[...], sh1, sc1).astype(BF16)
    proj = jnp.dot(h, win_ref[0], preferred_element_type=F32)
    u_b = proj[:, 0:CONV_WIDTH]
    u_c = proj[:, CONV_WIDTH:2 * CONV_WIDTH]
    u_h = proj[:, 2 * CONV_WIDTH:3 * CONV_WIDTH]
    v = proj[:, 3 * CONV_WIDTH:]
    if prev_moe:
        ps = ps + prev_rows.start_anchored(step + 1, 1)

    g = u_c * u_h
    gbuf[CONV_HALO:CONV_HALO + ts] = g
    vbuf[POOL_HALO:POOL_HALO + ts] = v

    conv = (cw[0:1] * gbuf[CONV_HALO - 2:CONV_HALO - 2 + ts]
            + cw[1:2] * gbuf[CONV_HALO - 1:CONV_HALO - 1 + ts]
            + cw[2:3] * g)
    y_conv = u_b * conv

    t_pos = (st * ts + lax.broadcasted_iota(jnp.int32, (ts, 1), 0) + 1).astype(F32)
    pooled = []
    for gi, win in enumerate(POOL_WINDOWS):
        lo = gi * POOL_GROUP
        v_g = v[:, lo:lo + POOL_GROUP]
        acc = vbuf[0:POOL_HALO + ts, lo:lo + POOL_GROUP]
        k = 1
        while k < win:
            acc = acc[k:] + acc[:-k]
            k *= 2
        first = POOL_HALO - win + 1
        inv_cnt = 1.0 / jnp.minimum(t_pos, float(win))
        pooled.append(acc[first:first + ts] * inv_cnt - v_g)
    pooled = jnp.concatenate(pooled, axis=1).astype(BF16)
    half = POOL_WIDTH // 2
    y_pool = jnp.concatenate(
        [jnp.dot(pooled[:, 0:half], pw_ref[0], preferred_element_type=F32),
         jnp.dot(pooled[:, half:], pw_ref[1], preferred_element_type=F32)], axis=1) * ps
    if prev_moe:
        g1 = g1 + prev_rows.start_anchored(step + 1, 2)

    gbuf[0:CONV_HALO] = gbuf[ts:ts + CONV_HALO]
    vbuf[0:POOL_HALO] = vbuf[ts:ts + POOL_HALO]

    y_mix = jnp.concatenate([y_conv, y_pool], axis=1).astype(BF16)
    x1 = x + g1 * jnp.dot(y_mix, wout_ref[0], preferred_element_type=F32)
    x1_ref[0] = x1
    if prev_moe:
        sh2 = sh2 + prev_rows.start_anchored(step + 1, 3)

    h2 = _norm_mod(x1, n2_ref[...], sh2, sc2)
    logits_t = lax.dot_general(wr_ref[...], h2.astype(BF16), (((1,), (1,)), ((), ())),
                               preferred_element_type=F32)
    info = _route(logits_t, rb_ref[...])
    ri_ref[...] = info

    packed = pltpu.pack_elementwise([h2[:, :D_MODEL // 2], h2[:, D_MODEL // 2:]], packed_dtype=BF16)
    for j in range(PACKED_SLABS):
        h_ref[pl.ds(j, ts, stride=SUBLANES), :] = packed[:, j * LANES:(j + 1) * LANES]
    info_t = jnp.concatenate([info, jnp.zeros((LANES - SUBLANES, ts), F32)], axis=0).T
    h_ref[pl.ds(INFO_SLAB, ts, stride=SUBLANES), :] = lax.bitcast_convert_type(info_t, U32)
    for j in range(INFO_SLAB + 1, SUBLANES):
        h_ref[pl.ds(j, ts, stride=SUBLANES), :] = jnp.zeros((ts, LANES), U32)

    if prev_moe:
        @pl.when(step == n_steps - 1)
        def _():
            prev_rows.wait(step + 1)


def _mix(layer, x, prev, mod, n1, n2, win, cw, pw, ps, wout, wr, rb):
    batch, seq, _ = x.shape
    of_layer = lambda shape: pl.BlockSpec((1,) + shape, lambda b, s, *_: (layer, 0, 0))
    ts = MIX_TOKENS
    n_s = seq // ts
    tokens = batch * seq
    const = lambda shape: pl.BlockSpec(shape, lambda b, s, *_: (0,) * len(shape))
    prev_specs = [] if prev is None else [
        pl.BlockSpec(memory_space=pl.ANY),
        pl.BlockSpec((1, 6, D_MODEL), lambda b, s, *_: (b, 0, 0))]
    return pl.pallas_call(
        functools.partial(_mix_kernel, prev_moe=prev is not None),
        out_shape=(
            jax.ShapeDtypeStruct((batch, seq, D_MODEL), F32),
            jax.ShapeDtypeStruct((tokens * SUBLANES, LANES), U32),
            jax.ShapeDtypeStruct((SUBLANES, tokens), F32),
        ),
        grid_spec=pltpu.PrefetchScalarGridSpec(
            num_scalar_prefetch=0 if prev is None else 1,
            grid=(batch, n_s),
            in_specs=[
                pl.BlockSpec((1, ts, D_MODEL), lambda b, s, *_: (b, s, 0)),
                *prev_specs,
                pl.BlockSpec((1, 6, D_MODEL), lambda b, s, *_: (b, 0, 0)),
                const((1, D_MODEL)), const((1, D_MODEL)),
                of_layer((D_MODEL, IN_COLS)),
                const((3, CONV_WIDTH)),
                const((2, POOL_WIDTH // 2, POOL_WIDTH // 2)),
                const((1, POOL_WIDTH)),
                of_layer((D_MODEL, D_MODEL)),
                const((N_EXPERTS, D_MODEL)),
                const((N_EXPERTS, 1)),
            ],
            out_specs=(
                pl.BlockSpec((1, ts, D_MODEL), lambda b, s, *_: (b, s, 0)),
                pl.BlockSpec((ts * SUBLANES, LANES), lambda b, s, *_: (b * n_s + s, 0)),
                pl.BlockSpec((SUBLANES, ts), lambda b, s, *_: (0, b * n_s + s)),
            ),
            scratch_shapes=[
                pltpu.VMEM((ts + CONV_HALO, CONV_WIDTH), F32),
                pltpu.VMEM((ts + POOL_HALO, POOL_WIDTH), F32),
                *([] if prev is None else _RowGather.scratch(ts)),
            ],
        ),
        compiler_params=pltpu.CompilerParams(
            dimension_semantics=("arbitrary", "arbitrary"), vmem_limit_bytes=VMEM_LIMIT),
        name="mix",
    )(*(() if prev is None else prev[:1]), x, *(() if prev is None else prev[1:]),
      mod, n1, n2, win, cw, pw, ps, wout, wr, rb)


def _dispatch_kernel(pos_ref, h_ref, z_ref, xs_hbm, sem, *, token_steps):
    n = DISPATCH_TOKENS
    i = pl.program_id(0)
    base = i * n

    def scatter(src_ref):
        for r in range(n):
            _row_copy(src_ref, r, xs_hbm, pos_ref[base + r], sem).start(priority=r % DMA_QUEUES)
        pltpu.make_async_copy(src_ref, xs_hbm.at[pl.ds(0, n * SUBLANES)], sem).wait()

    @pl.when(i < token_steps)
    def _():
        scatter(h_ref)

    @pl.when(i >= token_steps)
    def _():
        scatter(z_ref)


def _dispatch(pos_all, h_rows, n_rows):
    n = DISPATCH_TOKENS
    token_steps = h_rows.shape[0] // SUBLANES // n
    zero_rows = jnp.zeros((n * SUBLANES, LANES), U32)
    return pl.pallas_call(
        functools.partial(_dispatch_kernel, token_steps=token_steps),
        out_shape=jax.ShapeDtypeStruct((n_rows * SUBLANES, LANES), U32),
        grid_spec=pltpu.PrefetchScalarGridSpec(
            num_scalar_prefetch=1,
            grid=(pos_all.shape[0] // n,),
            in_specs=[pl.BlockSpec((n * SUBLANES, LANES),
                                   lambda i, p: (jnp.minimum(i, token_steps - 1), 0)),
                      pl.BlockSpec((n * SUBLANES, LANES), lambda i, p: (0, 0))],
            out_specs=pl.BlockSpec(memory_space=pl.ANY),
            scratch_shapes=[pltpu.SemaphoreType.DMA],
        ),
        compiler_params=pltpu.CompilerParams(
            dimension_semantics=("arbitrary",), vmem_limit_bytes=VMEM_LIMIT),
        name="dispatch",
    )(pos_all, h_rows, zero_rows)


def _ffn_kernel(ea_ref, eb_ref, valid_ref, xs_ref, wga_ref, wgb_ref, wda_ref, wdb_ref, y_ref):
    del ea_ref, eb_ref
    tm = FFN_ROWS
    i = pl.program_id(0)

    @pl.when(valid_ref[i] != 0)
    def _():
        slabs = [xs_ref[pl.ds(j, tm, stride=SUBLANES), :] for j in range(INFO_SLAB + 1)]
        packed = jnp.concatenate(slabs[:PACKED_SLABS], axis=1)
        xg = jnp.concatenate(
            [pltpu.unpack_elementwise(packed, index=k, packed_dtype=BF16, unpacked_dtype=F32)
             for k in range(2)], axis=1).astype(BF16)
        info = lax.bitcast_convert_type(slabs[INFO_SLAB], F32)

        def act(wg_ref, w):
            gu = jnp.dot(xg, wg_ref[0, 0], preferred_element_type=F32)
            gate = gu[:, :D_EXPERT]
            up = gu[:, D_EXPERT:]
            return (gate * jax.nn.sigmoid(gate) * up * w).astype(BF16)

        y = (jnp.dot(act(wga_ref, info[:, 1:2]), wda_ref[0, 0], preferred_element_type=F32)
             + jnp.dot(act(wgb_ref, info[:, 2:3]), wdb_ref[0, 0], preferred_element_type=F32))
        for j in range(SUBLANES):
            y_ref[pl.ds(j, tm, stride=SUBLANES), :] = y[:, j * LANES:(j + 1) * LANES]

    @pl.when(valid_ref[i] == 0)
    def _():
        y_ref[...] = jnp.zeros(y_ref.shape, F32)


def _ffn(layer, tile_ea, tile_eb, tile_valid, xs_rows, wgu, wd):
    n_tiles = tile_ea.shape[0]
    tm = FFN_ROWS
    return pl.pallas_call(
        _ffn_kernel,
        out_shape=jax.ShapeDtypeStruct((n_tiles * tm * SUBLANES, LANES), F32),
        grid_spec=pltpu.PrefetchScalarGridSpec(
            num_scalar_prefetch=3,
            grid=(n_tiles,),
            in_specs=[
                pl.BlockSpec((tm * SUBLANES, LANES), lambda i, ea, eb, va: (i, 0)),
                pl.BlockSpec((1, 1, D_MODEL, 2 * D_EXPERT),
                             lambda i, ea, eb, va: (layer, ea[i], 0, 0)),
                pl.BlockSpec((1, 1, D_MODEL, 2 * D_EXPERT),
                             lambda i, ea, eb, va: (layer, eb[i], 0, 0)),
                pl.BlockSpec((1, 1, D_EXPERT, D_MODEL), lambda i, ea, eb, va: (layer, ea[i], 0, 0)),
                pl.BlockSpec((1, 1, D_EXPERT, D_MODEL), lambda i, ea, eb, va: (layer, eb[i], 0, 0)),
            ],
            out_specs=pl.BlockSpec((tm * SUBLANES, LANES), lambda i, ea, eb, va: (i, 0)),
        ),
        compiler_params=pltpu.CompilerParams(
            dimension_semantics=("arbitrary",), vmem_limit_bytes=VMEM_LIMIT),
        name="ffn",
    )(tile_ea, tile_eb, tile_valid, xs_rows, wgu, wgu, wd, wd)


def _fin_kernel(pos_ref, x1_ref, mod_ref, fg_ref, y_hbm, o_ref, ybuf, ysem):
    tf = x1_ref.shape[1]
    step, n_steps = _linear_step()
    last_rows = _RowGather(pos_ref, y_hbm, ybuf, ysem, tf)

    @pl.when(step == 0)
    def _():
        last_rows.zero_spare()
        last_rows.start(0)

    last_rows.start(step + 1)
    last_rows.wait(step)
    x2 = x1_ref[0] + mod_ref[0][5:6] * _rows_to_tokens(last_rows.rows(step), tf)
    ms = jnp.mean(x2 * x2, axis=-1, keepdims=True)
    o_ref[0] = x2 * lax.rsqrt(ms + EPS) * fg_ref[...]

    @pl.when(step == n_steps - 1)
    def _():
        last_rows.wait(step + 1)


def _fin(pos, x1, y_rows, mod, final_g):
    batch, seq, _ = x1.shape
    tf = FIN_TOKENS
    return pl.pallas_call(
        _fin_kernel,
        out_shape=jax.ShapeDtypeStruct((batch, seq, D_MODEL), F32),
        grid_spec=pltpu.PrefetchScalarGridSpec(
            num_scalar_prefetch=1,
            grid=(batch, seq // tf),
            in_specs=[
                pl.BlockSpec((1, tf, D_MODEL), lambda b, s, p: (b, s, 0)),
                pl.BlockSpec((1, 6, D_MODEL), lambda b, s, p: (b, 0, 0)),
                pl.BlockSpec((1, D_MODEL), lambda b, s, p: (0, 0)),
                pl.BlockSpec(memory_space=pl.ANY),
            ],
            out_specs=pl.BlockSpec((1, tf, D_MODEL), lambda b, s, p: (b, s, 0)),
            scratch_shapes=_RowGather.scratch(tf),
        ),
        compiler_params=pltpu.CompilerParams(
            dimension_semantics=("arbitrary", "arbitrary"), vmem_limit_bytes=VMEM_LIMIT),
        name="fin",
    )(pos, x1, mod, final_g, y_rows)


def _plan(cls, n_tiles):
    tm = FFN_ROWS
    onehot = (cls[:, None] == jnp.arange(N_CLASSES, dtype=jnp.int32)[None, :]).astype(jnp.int32)
    csum = jnp.cumsum(onehot, axis=0)
    rank = jnp.take_along_axis(csum, cls[:, None], axis=1)[:, 0] - 1
    counts = csum[-1]
    tiles = (counts + tm - 1) // tm
    tile_end = jnp.cumsum(tiles)
    tile_start = tile_end - tiles
    pos = (tile_start[cls] * tm + rank).astype(jnp.int32)
    j = jnp.arange(n_tiles, dtype=jnp.int32)
    total = tile_end[-1]
    j_eff = jnp.minimum(j, total - 1)
    tile_cls = jnp.sum((tile_end[None, :] <= j_eff[:, None]).astype(jnp.int32), axis=1)
    group = tile_cls // N_PAIRS
    pair = tile_cls % N_PAIRS
    ea = group * GROUP_SIZE + jnp.asarray(PAIR_A, jnp.int32)[pair]
    eb = group * GROUP_SIZE + jnp.asarray(PAIR_B, jnp.int32)[pair]
    slot = jnp.arange(tm, dtype=jnp.int32)[None, :]
    is_pad = slot < (tiles * tm - counts)[:, None]
    spare_rank = (jnp.cumsum((~is_pad).reshape(-1).astype(jnp.int32)) - 1).reshape(N_CLASSES, tm)
    pad_pos = jnp.where(is_pad, (tile_start * tm + counts)[:, None] + slot, total * tm + spare_rank)
    pos_all = jnp.concatenate([pos, pad_pos.reshape(-1).astype(jnp.int32)])
    assert pos_all.shape[0] % DISPATCH_TOKENS == 0
    assert MIX_TOKENS == FIN_TOKENS
    pos_ahead = jnp.concatenate([pos, jnp.zeros((MIX_TOKENS,), jnp.int32)])
    return pos_all, pos_ahead, ea, eb, (j < total).astype(jnp.int32)


def kernel(x, c, w_in, conv_w, pool_w, pool_scale, w_out, norm1_g, norm2_g, w_ada, b_ada,
           w_router, router_bias, w_gate_up, w_down, final_g):
    batch, seq, _ = x.shape
    depth = w_in.shape[0]
    tokens = batch * seq
    n_tiles = tokens // FFN_ROWS + N_CLASSES

    mod = _ada(c, w_ada, b_ada).reshape(depth, batch, 6, D_MODEL)
    wr_t = w_router.T.astype(BF16)
    rb = router_bias.reshape(N_EXPERTS, 1).astype(F32)
    fg = final_g.reshape(1, D_MODEL)
    z = jnp.zeros((depth, POOL_GROUP, POOL_GROUP), F32)
    pw_blocks = jnp.stack([
        jnp.concatenate([jnp.concatenate([pool_w[:, 2 * k], z], axis=2),
                         jnp.concatenate([z, pool_w[:, 2 * k + 1]], axis=2)], axis=1)
        for k in range(2)], axis=1).astype(BF16)

    w_in_b, w_out_b = w_in.astype(BF16), w_out.astype(BF16)
    wgu_b, wd_b = w_gate_up.astype(BF16), w_down.astype(BF16)

    prev = None
    for l in range(depth):
        x, h_rows, info = _mix(
            l, x, prev, mod[l], norm1_g[l].reshape(1, -1), norm2_g[l].reshape(1, -1),
            w_in_b, conv_w[l], pw_blocks[l], pool_scale[l].reshape(1, -1), w_out_b, wr_t, rb)
        pos_all, pos_ahead, ea, eb, valid = _plan(info[0].astype(jnp.int32), n_tiles)
        xs_rows = _dispatch(pos_all, h_rows, n_tiles * FFN_ROWS)
        y_rows = _ffn(l, ea, eb, valid, xs_rows, wgu_b, wd_b)
        prev = (pos_ahead, y_rows, mod[l])
    return _fin(pos_ahead, x, y_rows, mod[depth - 1], fg)
```

```python
import functools

import jax
import jax.numpy as jnp
from jax import lax
from jax.experimental import pallas as pl
from jax.experimental.pallas import tpu as pltpu

D_MODEL = 1024
CONV_WIDTH = 512
POOL_WIDTH = 512
IN_COLS = 3 * CONV_WIDTH + POOL_WIDTH
POOL_WINDOWS = (2, 4, 8, 16)
POOL_GROUP = 128
N_EXPERTS = 16
N_GROUPS = 4
GROUP_SIZE = 4
D_EXPERT = 512
EPS = 1e-6

LANES = 128
SUBLANES = 8
PACKED_SLABS = D_MODEL // 2 // LANES
INFO_SLAB = PACKED_SLABS
CONV_HALO = SUBLANES
POOL_HALO = 2 * SUBLANES

PAIR_A = (0, 0, 0, 1, 1, 2)
PAIR_B = (1, 2, 3, 2, 3, 3)
N_PAIRS = len(PAIR_A)
N_CLASSES = N_GROUPS * N_PAIRS

MIX_TOKENS = 512
FFN_ROWS = 512
FIN_TOKENS = 512
DISPATCH_TOKENS = 1024
TOKEN_SLOTS = 2
DMA_GROUPS = 4
DMA_QUEUES = 2
VMEM_LIMIT = 56 * 1024 * 1024

F32 = jnp.float32
BF16 = jnp.bfloat16
U32 = jnp.uint32


def _norm_mod(v, gain, shift, scale):
    ms = jnp.mean(v * v, axis=-1, keepdims=True)
    return (v * lax.rsqrt(ms + EPS)) * (gain * (1.0 + scale)) + shift


def _rows_to_tokens(rows_ref, n):
    return jnp.concatenate(
        [rows_ref[pl.ds(j, n, stride=SUBLANES), :] for j in range(SUBLANES)], axis=1)


def _row_copy(src, src_row, dst, dst_row, sem):
    return pltpu.make_async_copy(
        src.at[pl.ds(pl.multiple_of(src_row * SUBLANES, SUBLANES), SUBLANES)],
        dst.at[pl.ds(pl.multiple_of(dst_row * SUBLANES, SUBLANES), SUBLANES)], sem)


class _RowGather:
    def __init__(self, idx_ref, src_hbm, buf, sems, n):
        self.idx_ref, self.src, self.buf, self.sems, self.n = idx_ref, src_hbm, buf, sems, n
        self.slots = buf.shape[0]

    def zero_spare(self):
        spare = (self.slots, SUBLANES, LANES)
        self.buf[:, self.n * SUBLANES:, :] = jnp.zeros(spare, self.buf.dtype)

    def start(self, tile, part=0, parts=1):
        slot = lax.rem(tile, self.slots)
        for r in range(part * self.n // parts, (part + 1) * self.n // parts):
            _row_copy(self.src, self.idx_ref[tile * self.n + r], self.buf.at[slot], r,
                      self.sems.at[slot]).start(priority=r % DMA_QUEUES)

    def start_anchored(self, tile, part):
        self.start(tile, part, DMA_GROUPS)
        spare = self.buf[lax.rem(tile + self.slots, self.slots), self.n * SUBLANES:, :]
        bits = lax.bitcast_convert_type(spare, U32)
        return lax.bitcast_convert_type(((bits >> 16) >> 16)[0:1, 0:1], F32)

    def wait(self, tile):
        slot = lax.rem(tile, self.slots)
        rows = pl.ds(0, self.n * SUBLANES)
        pltpu.make_async_copy(self.src.at[rows], self.buf.at[slot, rows], self.sems.at[slot]).wait()

    def rows(self, tile):
        return self.buf.at[lax.rem(tile, self.slots)]

    @staticmethod
    def scratch(n):
        return [pltpu.VMEM((TOKEN_SLOTS, (n + 1) * SUBLANES, LANES), F32),
                pltpu.SemaphoreType.DMA((TOKEN_SLOTS,))]


def _linear_step():
    return (pl.program_id(0) * pl.num_programs(1) + pl.program_id(1),
            pl.num_programs(0) * pl.num_programs(1))


def _ada_kernel(c_ref, w_ref, b_ref, o_ref):
    c = c_ref[...]
    c_act = (c * jax.nn.sigmoid(c)).astype(BF16)
    o_ref[0] = jnp.dot(c_act, w_ref[0].astype(BF16), preferred_element_type=F32) + b_ref[0]


def _ada(c, w_ada, b_ada):
    depth = w_ada.shape[0]
    batch = c.shape[0]
    n_col = w_ada.shape[2] // D_MODEL
    return pl.pallas_call(
        _ada_kernel,
        out_shape=jax.ShapeDtypeStruct((depth, batch, n_col * D_MODEL), F32),
        grid=(depth, n_col),
        in_specs=[
            pl.BlockSpec((batch, D_MODEL), lambda l, j: (0, 0)),
            pl.BlockSpec((1, D_MODEL, D_MODEL), lambda l, j: (l, 0, j)),
            pl.BlockSpec((1, 1, D_MODEL), lambda l, j: (l, 0, j)),
        ],
        out_specs=pl.BlockSpec((1, batch, D_MODEL), lambda l, j: (l, 0, j)),
        compiler_params=pltpu.CompilerParams(
            dimension_semantics=("arbitrary", "arbitrary"), vmem_limit_bytes=VMEM_LIMIT),
        name="ada",
    )(c, w_ada, b_ada.reshape(depth, 1, -1))


def _route(logits_t, bias):
    s = jax.nn.sigmoid(logits_t)
    sel = s + bias
    sel_r = [sel[e:e + 1] for e in range(N_EXPERTS)]
    s_r = [s[e:e + 1] for e in range(N_EXPERTS)]

    def group_score(g):
        r = sel_r[GROUP_SIZE * g:GROUP_SIZE * (g + 1)]
        best = r[PAIR_A[0]] + r[PAIR_B[0]]
        for p in range(1, N_PAIRS):
            best = jnp.maximum(best, r[PAIR_A[p]] + r[PAIR_B[p]])
        return best

    best_v = group_score(0)
    best_g = jnp.zeros_like(best_v)
    for g in range(1, N_GROUPS):
        gs = group_score(g)
        upd = gs > best_v
        best_v = jnp.where(upd, gs, best_v)
        best_g = jnp.where(upd, float(g), best_g)

    def pick(rows, i):
        out = rows[i]
        for g in range(1, N_GROUPS):
            out = jnp.where(best_g == float(g), rows[GROUP_SIZE * g + i], out)
        return out

    selg = [pick(sel_r, i) for i in range(GROUP_SIZE)]
    sg = [pick(s_r, i) for i in range(GROUP_SIZE)]
    chosen = []
    for i in range(GROUP_SIZE):
        beaten = jnp.zeros_like(best_v)
        for j in range(GROUP_SIZE):
            if j == i:
                continue
            wins = selg[j] > selg[i]
            if j < i:
                wins = wins | (selg[j] == selg[i])
            beaten = beaten + wins.astype(F32)
        chosen.append(beaten < 2.0)
    m0, m1, m2, m3 = chosen
    pair = jnp.where(m0, jnp.where(m1, 0.0, jnp.where(m2, 1.0, 2.0)),
                     jnp.where(m1, jnp.where(m2, 3.0, 4.0), 5.0))
    s_a = jnp.where(m0, sg[0], jnp.where(m1, sg[1], sg[2]))
    s_b = jnp.where(m3, sg[3], jnp.where(m2, sg[2], sg[1]))
    tot = s_a + s_b
    cls = best_g * float(N_PAIRS) + pair
    zero = jnp.zeros_like(cls)
    return jnp.concatenate([cls, s_a / tot, s_b / tot] + [zero] * (SUBLANES - 3), axis=0)


def _mix_kernel(*refs, prev_moe):
    if prev_moe:
        pos_ref, x_ref, y_hbm, pmod_ref, *refs = refs
        *refs, ybuf, ysem = refs
    else:
        x_ref, *refs = refs
    (mod_ref, n1_ref, n2_ref, win_ref, cw_ref, pw_ref, ps_ref, wout_ref, wr_ref, rb_ref,
     x1_ref, h_ref, ri_ref, gbuf, vbuf) = refs
    ts = x_ref.shape[1]
    st = pl.program_id(1)
    mod = mod_ref[0]
    sh1, sc1, g1, sh2, sc2 = (mod[i:i + 1] for i in range(5))
    cw = cw_ref[...]
    ps = ps_ref[...]
    x = x_ref[0]

    if prev_moe:
        step, n_steps = _linear_step()
        prev_rows = _RowGather(pos_ref, y_hbm, ybuf, ysem, ts)

        @pl.when(step == 0)
        def _():
            prev_rows.zero_spare()
            prev_rows.start(0)

        prev_rows.wait(step)
        x = x + pmod_ref[0][5:6] * _rows_to_tokens(prev_rows.rows(step), ts)
        cw = cw + prev_rows.start_anchored(step + 1, 0)

    @pl.when(st == 0)
    def _():
        gbuf[0:CONV_HALO] = jnp.zeros((CONV_HALO, CONV_WIDTH), F32)
        vbuf[0:POOL_HALO] = jnp.zeros((POOL_HALO, POOL_WIDTH), F32)

    h = _norm_mod(x, n1_ref[...], sh1, sc1).astype(BF16)
    proj = jnp.dot(h, win_ref[0], preferred_element_type=F32)
    u_b = proj[:, 0:CONV_WIDTH]
    u_c = proj[:, CONV_WIDTH:2 * CONV_WIDTH]
    u_h = proj[:, 2 * CONV_WIDTH:3 * CONV_WIDTH]
    v = proj[:, 3 * CONV_WIDTH:]
    if prev_moe:
        ps = ps + prev_rows.start_anchored(step + 1, 1)

    g = u_c * u_h
    gbuf[CONV_HALO:CONV_HALO + ts] = g
    vbuf[POOL_HALO:POOL_HALO + ts] = v

    conv = (cw[0:1] * gbuf[CONV_HALO - 2:CONV_HALO - 2 + ts]
            + cw[1:2] * gbuf[CONV_HALO - 1:CONV_HALO - 1 + ts]
            + cw[2:3] * g)
    y_conv = u_b * conv

    t_pos = (st * ts + lax.broadcasted_iota(jnp.int32, (ts, 1), 0) + 1).astype(F32)
    pooled = []
    for gi, win in enumerate(POOL_WINDOWS):
        lo = gi * POOL_GROUP
        v_g = v[:, lo:lo + POOL_GROUP]
        acc = vbuf[0:POOL_HALO + ts, lo:lo + POOL_GROUP]
        k = 1
        while k < win:
            acc = acc[k:] + acc[:-k]
            k *= 2
        first = POOL_HALO - win + 1
        inv_cnt = 1.0 / jnp.minimum(t_pos, float(win))
        pooled.append(acc[first:first + ts] * inv_cnt - v_g)
    pooled = jnp.concatenate(pooled, axis=1).astype(BF16)
    half = POOL_WIDTH // 2
    y_pool = jnp.concatenate(
        [jnp.dot(pooled[:, 0:half], pw_ref[0], preferred_element_type=F32),
         jnp.dot(pooled[:, half:], pw_ref[1], preferred_element_type=F32)], axis=1) * ps
    if prev_moe:
        g1 = g1 + prev_rows.start_anchored(step + 1, 2)

    gbuf[0:CONV_HALO] = gbuf[ts:ts + CONV_HALO]
    vbuf[0:POOL_HALO] = vbuf[ts:ts + POOL_HALO]

    y_mix = jnp.concatenate([y_conv, y_pool], axis=1).astype(BF16)
    x1 = x + g1 * jnp.dot(y_mix, wout_ref[0], preferred_element_type=F32)
    x1_ref[0] = x1
    if prev_moe:
        sh2 = sh2 + prev_rows.start_anchored(step + 1, 3)

    h2 = _norm_mod(x1, n2_ref[...], sh2, sc2)
    logits_t = lax.dot_general(wr_ref[...], h2.astype(BF16), (((1,), (1,)), ((), ())),
                               preferred_element_type=F32)
    info = _route(logits_t, rb_ref[...])
    ri_ref[...] = info

    packed = pltpu.pack_elementwise([h2[:, :D_MODEL // 2], h2[:, D_MODEL // 2:]], packed_dtype=BF16)
    for j in range(PACKED_SLABS):
        h_ref[pl.ds(j, ts, stride=SUBLANES), :] = packed[:, j * LANES:(j + 1) * LANES]
    info_t = jnp.concatenate([info, jnp.zeros((LANES - SUBLANES, ts), F32)], axis=0).T
    h_ref[pl.ds(INFO_SLAB, ts, stride=SUBLANES), :] = lax.bitcast_convert_type(info_t, U32)
    for j in range(INFO_SLAB + 1, SUBLANES):
        h_ref[pl.ds(j, ts, stride=SUBLANES), :] = jnp.zeros((ts, LANES), U32)

    if prev_moe:
        @pl.when(step == n_steps - 1)
        def _():
            prev_rows.wait(step + 1)


def _mix(layer, x, prev, mod, n1, n2, win, cw, pw, ps, wout, wr, rb):
    batch, seq, _ = x.shape
    of_layer = lambda shape: pl.BlockSpec((1,) + shape, lambda b, s, *_: (layer, 0, 0))
    ts = MIX_TOKENS
    n_s = seq // ts
    tokens = batch * seq
    const = lambda shape: pl.BlockSpec(shape, lambda b, s, *_: (0,) * len(shape))
    prev_specs = [] if prev is None else [
        pl.BlockSpec(memory_space=pl.ANY),
        pl.BlockSpec((1, 6, D_MODEL), lambda b, s, *_: (b, 0, 0))]
    return pl.pallas_call(
        functools.partial(_mix_kernel, prev_moe=prev is not None),
        out_shape=(
            jax.ShapeDtypeStruct((batch, seq, D_MODEL), F32),
            jax.ShapeDtypeStruct((tokens * SUBLANES, LANES), U32),
            jax.ShapeDtypeStruct((SUBLANES, tokens), F32),
        ),
        grid_spec=pltpu.PrefetchScalarGridSpec(
            num_scalar_prefetch=0 if prev is None else 1,
            grid=(batch, n_s),
            in_specs=[
                pl.BlockSpec((1, ts, D_MODEL), lambda b, s, *_: (b, s, 0)),
                *prev_specs,
                pl.BlockSpec((1, 6, D_MODEL), lambda b, s, *_: (b, 0, 0)),
                const((1, D_MODEL)), const((1, D_MODEL)),
                of_layer((D_MODEL, IN_COLS)),
                const((3, CONV_WIDTH)),
                const((2, POOL_WIDTH // 2, POOL_WIDTH // 2)),
                const((1, POOL_WIDTH)),
                of_layer((D_MODEL, D_MODEL)),
                const((N_EXPERTS, D_MODEL)),
                const((N_EXPERTS, 1)),
            ],
            out_specs=(
                pl.BlockSpec((1, ts, D_MODEL), lambda b, s, *_: (b, s, 0)),
                pl.BlockSpec((ts * SUBLANES, LANES), lambda b, s, *_: (b * n_s + s, 0)),
                pl.BlockSpec((SUBLANES, ts), lambda b, s, *_: (0, b * n_s + s)),
            ),
            scratch_shapes=[
                pltpu.VMEM((ts + CONV_HALO, CONV_WIDTH), F32),
                pltpu.VMEM((ts + POOL_HALO, POOL_WIDTH), F32),
                *([] if prev is None else _RowGather.scratch(ts)),
            ],
        ),
        compiler_params=pltpu.CompilerParams(
            dimension_semantics=("arbitrary", "arbitrary"), vmem_limit_bytes=VMEM_LIMIT),
        name="mix",
    )(*(() if prev is None else prev[:1]), x, *(() if prev is None else prev[1:]),
      mod, n1, n2, win, cw, pw, ps, wout, wr, rb)


def _dispatch_kernel(pos_ref, h_ref, z_ref, xs_hbm, sem, *, token_steps):
    n = DISPATCH_TOKENS
    i = pl.program_id(0)
    base = i * n

    def scatter(src_ref):
        for r in range(n):
            _row_copy(src_ref, r, xs_hbm, pos_ref[base + r], sem).start(priority=r % DMA_QUEUES)
        pltpu.make_async_copy(src_ref, xs_hbm.at[pl.ds(0, n * SUBLANES)], sem).wait()

    @pl.when(i < token_steps)
    def _():
        scatter(h_ref)

    @pl.when(i >= token_steps)
    def _():
        scatter(z_ref)


def _dispatch(pos_all, h_rows, n_rows):
    n = DISPATCH_TOKENS
    token_steps = h_rows.shape[0] // SUBLANES // n
    zero_rows = jnp.zeros((n * SUBLANES, LANES), U32)
    return pl.pallas_call(
        functools.partial(_dispatch_kernel, token_steps=token_steps),
        out_shape=jax.ShapeDtypeStruct((n_rows * SUBLANES, LANES), U32),
        grid_spec=pltpu.PrefetchScalarGridSpec(
            num_scalar_prefetch=1,
            grid=(pos_all.shape[0] // n,),
            in_specs=[pl.BlockSpec((n * SUBLANES, LANES),
                                   lambda i, p: (jnp.minimum(i, token_steps - 1), 0)),
                      pl.BlockSpec((n * SUBLANES, LANES), lambda i, p: (0, 0))],
            out_specs=pl.BlockSpec(memory_space=pl.ANY),
            scratch_shapes=[pltpu.SemaphoreType.DMA],
        ),
        compiler_params=pltpu.CompilerParams(
            dimension_semantics=("arbitrary",), vmem_limit_bytes=VMEM_LIMIT),
        name="dispatch",
    )(pos_all, h_rows, zero_rows)


def _ffn_kernel(ea_ref, eb_ref, valid_ref, xs_ref, wga_ref, wgb_ref, wda_ref, wdb_ref, y_ref):
    del ea_ref, eb_ref
    tm = FFN_ROWS
    i = pl.program_id(0)

    @pl.when(valid_ref[i] != 0)
    def _():
        slabs = [xs_ref[pl.ds(j, tm, stride=SUBLANES), :] for j in range(INFO_SLAB + 1)]
        packed = jnp.concatenate(slabs[:PACKED_SLABS], axis=1)
        xg = jnp.concatenate(
            [pltpu.unpack_elementwise(packed, index=k, packed_dtype=BF16, unpacked_dtype=F32)
             for k in range(2)], axis=1).astype(BF16)
        info = lax.bitcast_convert_type(slabs[INFO_SLAB], F32)

        def act(wg_ref, w):
            gu = jnp.dot(xg, wg_ref[0, 0], preferred_element_type=F32)
            gate = gu[:, :D_EXPERT]
            up = gu[:, D_EXPERT:]
            return (gate * jax.nn.sigmoid(gate) * up * w).astype(BF16)

        y = (jnp.dot(act(wga_ref, info[:, 1:2]), wda_ref[0, 0], preferred_element_type=F32)
             + jnp.dot(act(wgb_ref, info[:, 2:3]), wdb_ref[0, 0], preferred_element_type=F32))
        for j in range(SUBLANES):
            y_ref[pl.ds(j, tm, stride=SUBLANES), :] = y[:, j * LANES:(j + 1) * LANES]

    @pl.when(valid_ref[i] == 0)
    def _():
        y_ref[...] = jnp.zeros(y_ref.shape, F32)


def _ffn(layer, tile_ea, tile_eb, tile_valid, xs_rows, wgu, wd):
    n_tiles = tile_ea.shape[0]
    tm = FFN_ROWS
    return pl.pallas_call(
        _ffn_kernel,
        out_shape=jax.ShapeDtypeStruct((n_tiles * tm * SUBLANES, LANES), F32),
        grid_spec=pltpu.PrefetchScalarGridSpec(
            num_scalar_prefetch=3,
            grid=(n_tiles,),
            in_specs=[
                pl.BlockSpec((tm * SUBLANES, LANES), lambda i, ea, eb, va: (i, 0)),
                pl.BlockSpec((1, 1, D_MODEL, 2 * D_EXPERT),
                             lambda i, ea, eb, va: (layer, ea[i], 0, 0)),
                pl.BlockSpec((1, 1, D_MODEL, 2 * D_EXPERT),
                             lambda i, ea, eb, va: (layer, eb[i], 0, 0)),
                pl.BlockSpec((1, 1, D_EXPERT, D_MODEL), lambda i, ea, eb, va: (layer, ea[i], 0, 0)),
                pl.BlockSpec((1, 1, D_EXPERT, D_MODEL), lambda i, ea, eb, va: (layer, eb[i], 0, 0)),
            ],
            out_specs=pl.BlockSpec((tm * SUBLANES, LANES), lambda i, ea, eb, va: (i, 0)),
        ),
        compiler_params=pltpu.CompilerParams(
            dimension_semantics=("arbitrary",), vmem_limit_bytes=VMEM_LIMIT),
        name="ffn",
    )(tile_ea, tile_eb, tile_valid, xs_rows, wgu, wgu, wd, wd)


def _fin_kernel(pos_ref, x1_ref, mod_ref, fg_ref, y_hbm, o_ref, ybuf, ysem):
    tf = x1_ref.shape[1]
    step, n_steps = _linear_step()
    last_rows = _RowGather(pos_ref, y_hbm, ybuf, ysem, tf)

    @pl.when(step == 0)
    def _():
        last_rows.zero_spare()
        last_rows.start(0)

    last_rows.start(step + 1)
    last_rows.wait(step)
    x2 = x1_ref[0] + mod_ref[0][5:6] * _rows_to_tokens(last_rows.rows(step), tf)
    ms = jnp.mean(x2 * x2, axis=-1, keepdims=True)
    o_ref[0] = x2 * lax.rsqrt(ms + EPS) * fg_ref[...]

    @pl.when(step == n_steps - 1)
    def _():
        last_rows.wait(step + 1)


def _fin(pos, x1, y_rows, mod, final_g):
    batch, seq, _ = x1.shape
    tf = FIN_TOKENS
    return pl.pallas_call(
        _fin_kernel,
        out_shape=jax.ShapeDtypeStruct((batch, seq, D_MODEL), F32),
        grid_spec=pltpu.PrefetchScalarGridSpec(
            num_scalar_prefetch=1,
            grid=(batch, seq // tf),
            in_specs=[
                pl.BlockSpec((1, tf, D_MODEL), lambda b, s, p: (b, s, 0)),
                pl.BlockSpec((1, 6, D_MODEL), lambda b, s, p: (b, 0, 0)),
                pl.BlockSpec((1, D_MODEL), lambda b, s, p: (0, 0)),
                pl.BlockSpec(memory_space=pl.ANY),
            ],
            out_specs=pl.BlockSpec((1, tf, D_MODEL), lambda b, s, p: (b, s, 0)),
            scratch_shapes=_RowGather.scratch(tf),
        ),
        compiler_params=pltpu.CompilerParams(
            dimension_semantics=("arbitrary", "arbitrary"), vmem_limit_bytes=VMEM_LIMIT),
        name="fin",
    )(pos, x1, mod, final_g, y_rows)


def _plan(cls, n_tiles):
    tm = FFN_ROWS
    onehot = (cls[:, None] == jnp.arange(N_CLASSES, dtype=jnp.int32)[None, :]).astype(jnp.int32)
    csum = jnp.cumsum(onehot, axis=0)
    rank = jnp.take_along_axis(csum, cls[:, None], axis=1)[:, 0] - 1
    counts = csum[-1]
    tiles = (counts + tm - 1) // tm
    tile_end = jnp.cumsum(tiles)
    tile_start = tile_end - tiles
    pos = (tile_start[cls] * tm + rank).astype(jnp.int32)
    j = jnp.arange(n_tiles, dtype=jnp.int32)
    total = tile_end[-1]
    j_eff = jnp.minimum(j, total - 1)
    tile_cls = jnp.sum((tile_end[None, :] <= j_eff[:, None]).astype(jnp.int32), axis=1)
    group = tile_cls // N_PAIRS
    pair = tile_cls % N_PAIRS
    ea = group * GROUP_SIZE + jnp.asarray(PAIR_A, jnp.int32)[pair]
    eb = group * GROUP_SIZE + jnp.asarray(PAIR_B, jnp.int32)[pair]
    slot = jnp.arange(tm, dtype=jnp.int32)[None, :]
    is_pad = slot < (tiles * tm - counts)[:, None]
    spare_rank = (jnp.cumsum((~is_pad).reshape(-1).astype(jnp.int32)) - 1).reshape(N_CLASSES, tm)
    pad_pos = jnp.where(is_pad, (tile_start * tm + counts)[:, None] + slot, total * tm + spare_rank)
    pos_all = jnp.concatenate([pos, pad_pos.reshape(-1).astype(jnp.int32)])
    assert pos_all.shape[0] % DISPATCH_TOKENS == 0
    assert MIX_TOKENS == FIN_TOKENS
    pos_ahead = jnp.concatenate([pos, jnp.zeros((MIX_TOKENS,), jnp.int32)])
    return pos_all, pos_ahead, ea, eb, (j < total).astype(jnp.int32)


def kernel(x, c, w_in, conv_w, pool_w, pool_scale, w_out, norm1_g, norm2_g, w_ada, b_ada,
           w_router, router_bias, w_gate_up, w_down, final_g):
    batch, seq, _ = x.shape
    depth = w_in.shape[0]
    tokens = batch * seq
    n_tiles = tokens // FFN_ROWS + N_CLASSES

    mod = _ada(c, w_ada, b_ada).reshape(depth, batch, 6, D_MODEL)
    wr_t = w_router.T.astype(BF16)
    rb = router_bias.reshape(N_EXPERTS, 1).astype(F32)
    fg = final_g.reshape(1, D_MODEL)
    z = jnp.zeros((depth, POOL_GROUP, POOL_GROUP), F32)
    pw_blocks = jnp.stack([
        jnp.concatenate([jnp.concatenate([pool_w[:, 2 * k], z], axis=2),
                         jnp.concatenate([z, pool_w[:, 2 * k + 1]], axis=2)], axis=1)
        for k in range(2)], axis=1).astype(BF16)

    w_in_b, w_out_b = w_in.astype(BF16), w_out.astype(BF16)
    wgu_b, wd_b = w_gate_up.astype(BF16), w_down.astype(BF16)

    prev = None
    for l in range(depth):
        x, h_rows, info = _mix(
            l, x, prev, mod[l], norm1_g[l].reshape(1, -1), norm2_g[l].reshape(1, -1),
            w_in_b, conv_w[l], pw_blocks[l], pool_scale[l].reshape(1, -1), w_out_b, wr_t, rb)
        pos_all, pos_ahead, ea, eb, valid = _plan(info[0].astype(jnp.int32), n_tiles)
        xs_rows = _dispatch(pos_all, h_rows, n_tiles * FFN_ROWS)
        y_rows = _ffn(l, ea, eb, valid, xs_rows, wgu_b, wd_b)
        prev = (pos_ahead, y_rows, mod[l])
    return _fin(pos_ahead, x, y_rows, mod[depth - 1], fg)
```
